```python
import jax
import jax.numpy as jnp
from jax import lax
import numpy as np

D_MODEL = 1024
BATCH = 8
SEQ = 4096
DEPTH = 1

PLE_DIM = 256
D_FF = 2816
RET_HEADS = 4
RET_QK_DIM = 64
RET_V_DIM = 128
RET_CHUNK = 128
SWA_Q_HEADS = 8
SWA_KV_HEADS = 2
SWA_HEAD_DIM = 64
SWA_WINDOW = 128
ROPE_BASE = 10000.0
EPS = 1e-6
NEG_INF = -1e30

RET_QK_W = RET_HEADS * RET_QK_DIM
RET_V_W = RET_HEADS * RET_V_DIM
SWA_Q_W = SWA_Q_HEADS * SWA_HEAD_DIM
SWA_KV_W = SWA_KV_HEADS * SWA_HEAD_DIM
MIX_W = RET_V_W + SWA_Q_W
IN_W = 2 * RET_QK_W + 2 * RET_V_W + SWA_Q_W + 2 * SWA_KV_W
SPLIT_POINTS = (
    RET_QK_W,
    2 * RET_QK_W,
    2 * RET_QK_W + RET_V_W,
    2 * RET_QK_W + 2 * RET_V_W,
    2 * RET_QK_W + 2 * RET_V_W + SWA_Q_W,
    2 * RET_QK_W + 2 * RET_V_W + SWA_Q_W + SWA_KV_W,
)

kernel_name = 'hymba_style_retention_swa_sink_macaron'


def _rmsnorm(x, g):
    x32 = x.astype(jnp.float32)
    y = x32 * lax.rsqrt(jnp.mean(x32 * x32, axis=-1, keepdims=True) + EPS)
    return y.astype(x.dtype) * g


def _swiglu(x, w_gate, w_up, w_down):
    return (jax.nn.silu(x @ w_gate) * (x @ w_up)) @ w_down


def _rotary(x, pos):
    half = x.shape[-1] // 2
    inv_freq = ROPE_BASE ** (-jnp.arange(half, dtype=jnp.float32) / half)
    ang = pos.astype(jnp.float32)[:, :, None, None] * inv_freq
    cos, sin = jnp.cos(ang), jnp.sin(ang)
    x32 = x.astype(jnp.float32)
    x1, x2 = x32[..., :half], x32[..., half:]
    return jnp.concatenate([x1 * cos - x2 * sin, x2 * cos + x1 * sin], axis=-1).astype(x.dtype)


def _head_groupnorm(y):
    y32 = y.astype(jnp.float32)
    mu = jnp.mean(y32, axis=-1, keepdims=True)
    var = jnp.mean(jnp.square(y32 - mu), axis=-1, keepdims=True)
    return ((y32 - mu) * lax.rsqrt(var + EPS)).astype(y.dtype)


def _retention_chunkwise(q, k, v):
    B, S, H, dk = q.shape
    dv = v.shape[-1]
    C = RET_CHUNK
    N = S // C
    log_gamma = jnp.log(1.0 - 2.0 ** (-5.0 - jnp.arange(H, dtype=jnp.float32)))
    idx = jnp.arange(C, dtype=jnp.float32)
    diff = idx[:, None] - idx[None, :]
    causal = diff >= 0
    decay_intra = jnp.where(causal[None], jnp.exp(log_gamma[:, None, None] * jnp.where(causal, diff, 0.0)[None]), 0.0)
    q_decay = jnp.exp(log_gamma[:, None] * (idx[None, :] + 1.0))
    k_decay = jnp.exp(log_gamma[:, None] * (C - 1.0 - idx[None, :]))
    chunk_decay = jnp.exp(log_gamma * C)

    qc = q.astype(jnp.float32).reshape(B, N, C, H, dk)
    kc = k.astype(jnp.float32).reshape(B, N, C, H, dk) * (dk ** -0.5)
    vc = v.astype(jnp.float32).reshape(B, N, C, H, dv)

    scores = jnp.einsum('bnchd,bnmhd->bnhcm', qc, kc) * decay_intra
    intra = jnp.einsum('bnhcm,bnmhe->bnche', scores, vc)
    kv = jnp.einsum('bnmhd,hm,bnmhe->bnhde', kc, k_decay, vc)

    def step(state, kv_n):
        return chunk_decay[:, None, None] * state + kv_n, state

    init = jnp.zeros((B, H, dk, dv), jnp.float32)
    _, state_prev = lax.scan(step, init, jnp.moveaxis(kv, 1, 0))
    state_prev = jnp.moveaxis(state_prev, 0, 1)
    cross = jnp.einsum('bnchd,hc,bnhde->bnche', qc, q_decay, state_prev)
    return (intra + cross).reshape(B, S, H, dv).astype(v.dtype)


def _swa_with_sinks(q, k, v, sinks):
    B, S, Hq, D = q.shape
    Hkv = k.shape[2]
    G = Hq // Hkv
    W = SWA_WINDOW
    N = S // W
    qb = q.reshape(B, N, W, Hkv, G, D)
    pad = ((0, 0), (1, 0), (0, 0), (0, 0), (0, 0))
    kp = jnp.pad(k.reshape(B, N, W, Hkv, D), pad)
    vp = jnp.pad(v.reshape(B, N, W, Hkv, D), pad)
    kwin = jnp.concatenate([kp[:, :-1], kp[:, 1:]], axis=2)
    vwin = jnp.concatenate([vp[:, :-1], vp[:, 1:]], axis=2)

    scores = jnp.einsum('bnqhgd,bnkhd->bnhgqk', qb, kwin).astype(jnp.float32) * (D ** -0.5)
    qi = jnp.arange(W)[:, None] + W
    kj = jnp.arange(2 * W)[None, :]
    rel = qi - kj
    valid = (rel >= 0) & (rel < W)
    blk_valid = valid[None] & ((jnp.arange(N)[:, None, None] > 0) | (kj >= W)[None])
    scores = jnp.where(blk_valid[None, :, None, None], scores, NEG_INF)

    sink = sinks.astype(jnp.float32).reshape(Hkv, G)[None, None, :, :, None, None]
    m = jnp.maximum(jnp.max(scores, axis=-1, keepdims=True), sink)
    e = jnp.exp(scores - m)
    denom = jnp.sum(e, axis=-1, keepdims=True) + jnp.exp(sink - m)
    probs = (e / denom).astype(v.dtype)
    out = jnp.einsum('bnhgqk,bnkhd->bnqhgd', probs, vwin)
    return out.reshape(B, S, Hq * D)


def setup_inputs(seed: int = 0) -> dict:
    key = jax.random.key(seed)
    ks = jax.random.split(key, 24)
    f32 = jnp.float32

    def w(k, shape, fan_in):
        return jax.random.normal(k, shape, f32) * (fan_in ** -0.5)

    def gain(k):
        return 1.0 + 0.02 * jax.random.normal(k, (DEPTH, D_MODEL), f32)

    x = jax.random.normal(ks[0], (BATCH, SEQ, D_MODEL), f32)
    p = jax.random.normal(ks[1], (DEPTH, BATCH, SEQ, PLE_DIM), f32)
    positions = jnp.tile(jnp.arange(SEQ, dtype=jnp.int32)[None, :], (BATCH, 1))
    return {
        'x': x,
        'p': p,
        'positions': positions,
        'ffn1_pre_g': gain(ks[2]),
        'ffn1_post_g': gain(ks[3]),
        'ffn1_w_gate': w(ks[4], (DEPTH, D_MODEL, D_FF), D_MODEL),
        'ffn1_w_up': w(ks[5], (DEPTH, D_MODEL, D_FF), D_MODEL),
        'ffn1_w_down': w(ks[6], (DEPTH, D_FF, D_MODEL), D_FF),
        'mix_pre_g': gain(ks[7]),
        'mix_post_g': gain(ks[8]),
        'w_in': w(ks[9], (DEPTH, D_MODEL, IN_W), D_MODEL),
        'b_in': 0.02 * jax.random.normal(ks[10], (DEPTH, IN_W), f32),
        'swa_sinks': 0.5 * jax.random.normal(ks[11], (DEPTH, SWA_Q_HEADS), f32),
        'w_out': w(ks[12], (DEPTH, MIX_W, D_MODEL), MIX_W),
        'ffn2_pre_g': gain(ks[13]),
        'ffn2_post_g': gain(ks[14]),
        'ffn2_w_gate': w(ks[15], (DEPTH, D_MODEL, D_FF), D_MODEL),
        'ffn2_w_up': w(ks[16], (DEPTH, D_MODEL, D_FF), D_MODEL),
        'ffn2_w_down': w(ks[17], (DEPTH, D_FF, D_MODEL), D_FF),
        'ple_w_proj': w(ks[18], (DEPTH, PLE_DIM, D_MODEL), PLE_DIM),
        'ple_w_gate': w(ks[19], (DEPTH, D_MODEL, D_MODEL), D_MODEL),
        'ple_norm_g': gain(ks[20]),
    }


def reference(x, p, positions, ffn1_pre_g, ffn1_post_g, ffn1_w_gate, ffn1_w_up, ffn1_w_down,
              mix_pre_g, mix_post_g, w_in, b_in, swa_sinks, w_out,
              ffn2_pre_g, ffn2_post_g, ffn2_w_gate, ffn2_w_up, ffn2_w_down,
              ple_w_proj, ple_w_gate, ple_norm_g):
    B, S, _ = x.shape
    h = x
    for i in range(DEPTH):
        a = _rmsnorm(h, ffn1_pre_g[i])
        h = h + 0.5 * _rmsnorm(_swiglu(a, ffn1_w_gate[i], ffn1_w_up[i], ffn1_w_down[i]), ffn1_post_g[i])

        u = _rmsnorm(h, mix_pre_g[i])
        z = u @ w_in[i] + b_in[i]
        rq, rk, rv, rg, sq, sk, sv = jnp.split(z, SPLIT_POINTS, axis=-1)

        rq = _rotary(rq.reshape(B, S, RET_HEADS, RET_QK_DIM), positions)
        rk = _rotary(rk.reshape(B, S, RET_HEADS, RET_QK_DIM), positions)
        rv = rv.reshape(B, S, RET_HEADS, RET_V_DIM)
        ret = _head_groupnorm(_retention_chunkwise(rq, rk, rv)).reshape(B, S, RET_V_W)
        ret = jax.nn.silu(rg) * ret

        swa = _swa_with_sinks(sq.reshape(B, S, SWA_Q_HEADS, SWA_HEAD_DIM),
                              sk.reshape(B, S, SWA_KV_HEADS, SWA_HEAD_DIM),
                              sv.reshape(B, S, SWA_KV_HEADS, SWA_HEAD_DIM),
                              swa_sinks[i])

        mix = jnp.concatenate([ret, swa], axis=-1) @ w_out[i]
        h = h + _rmsnorm(mix, mix_post_g[i])

        a = _rmsnorm(h, ffn2_pre_g[i])
        h = h + 0.5 * _rmsnorm(_swiglu(a, ffn2_w_gate[i], ffn2_w_up[i], ffn2_w_down[i]), ffn2_post_g[i])

        gate = jax.nn.sigmoid(h @ ple_w_gate[i])
        h = h + _rmsnorm(gate * (p[i] @ ple_w_proj[i]), ple_norm_g[i])
    return h
```

```python
import functools
import math

import jax
import jax.numpy as jnp
from jax import lax
from jax.experimental import pallas as pl
from jax.experimental.pallas import tpu as pltpu

D_MODEL = 1024
PLE_DIM = 256
D_FF = 2816
RET_HEADS = 4
RET_QK_DIM = 64
RET_V_DIM = 128
CHUNK = 128
SWA_Q_HEADS = 8
SWA_KV_HEADS = 2
SWA_HEAD_DIM = 64
ROPE_BASE = 10000.0
EPS = 1e-6
NEG_INF = -1e30

RET_QK_W = RET_HEADS * RET_QK_DIM
RET_V_W = RET_HEADS * RET_V_DIM
SWA_Q_W = SWA_Q_HEADS * SWA_HEAD_DIM
SWA_KV_W = SWA_KV_HEADS * SWA_HEAD_DIM
IN_W = 2 * RET_QK_W + 2 * RET_V_W + SWA_Q_W + 2 * SWA_KV_W
OFF_RQ = 0
OFF_RK = OFF_RQ + RET_QK_W
OFF_RV = OFF_RK + RET_QK_W
OFF_RG = OFF_RV + RET_V_W
OFF_SQ = OFF_RG + RET_V_W
OFF_SK = OFF_SQ + SWA_Q_W
OFF_SV = OFF_SK + SWA_KV_W

LANES = 128
FF_CHUNK = 256
TM_FFN = 512
TM_MIX = 512
VMEM_LIMIT = 56 * 1024 * 1024

BF16 = jnp.bfloat16
F32 = jnp.float32


def _rms(x, g):
    return x * lax.rsqrt(jnp.mean(x * x, axis=-1, keepdims=True) + EPS) * g


def _const_spec(shape):
    nd = len(shape)
    return pl.BlockSpec(shape, lambda *_: (0,) * nd, pipeline_mode=pl.Buffered(1))


def _ffn_kernel(x_ref, pre_g_ref, post_g_ref, wg_ref, wu_ref, wd_ref, *rest, with_ple):
    if with_ple:
        p_ref, wpg_ref, wpp_ref, ple_g_ref, o_ref, acc_ref = rest
    else:
        o_ref, acc_ref = rest
    x = x_ref[...]
    a = _rms(x, pre_g_ref[...]).astype(BF16)
    n_chunks = wg_ref.shape[0]
    for c in range(n_chunks):
        g = jnp.dot(a, wg_ref[c], preferred_element_type=F32)
        u = jnp.dot(a, wu_ref[c], preferred_element_type=F32)
        hm = (g * jax.nn.sigmoid(g) * u).astype(BF16)
        d = jnp.dot(hm, wd_ref[c], preferred_element_type=F32)
        if c == 0:
            acc_ref[...] = d
        else:
            acc_ref[...] += d
    h = x + 0.5 * _rms(acc_ref[...], post_g_ref[...])
    if with_ple:
        gate = jax.nn.sigmoid(jnp.dot(h.astype(BF16), wpg_ref[...], preferred_element_type=F32))
        proj = jnp.dot(p_ref[...].astype(BF16), wpp_ref[...], preferred_element_type=F32)
        h = h + _rms(gate * proj, ple_g_ref[...])
    o_ref[...] = h


def _ffn_call(x, pre_g, post_g, wg, wu, wd, ple=None, *, name):
    t, d = x.shape
    n_chunks = D_FF // FF_CHUNK
    tile = pl.BlockSpec((TM_FFN, d), lambda i: (i, 0))
    in_specs = [tile, _const_spec((1, d)), _const_spec((1, d)),
                _const_spec((n_chunks, d, FF_CHUNK)), _const_spec((n_chunks, d, FF_CHUNK)),
                _const_spec((n_chunks, FF_CHUNK, d))]
    args = [x, pre_g, post_g, wg, wu, wd]
    if ple is not None:
        p, wpg, wpp, ple_g = ple
        in_specs += [pl.BlockSpec((TM_FFN, PLE_DIM), lambda i: (i, 0)),
                     _const_spec((d, d)), _const_spec((PLE_DIM, d)), _const_spec((1, d))]
        args += [p, wpg, wpp, ple_g]
    return pl.pallas_call(
        functools.partial(_ffn_kernel, with_ple=ple is not None),
        grid=(t // TM_FFN,),
        in_specs=in_specs,
        out_specs=tile,
        out_shape=jax.ShapeDtypeStruct((t, d), F32),
        scratch_shapes=[pltpu.VMEM((TM_FFN, d), F32)],
        compiler_params=pltpu.CompilerParams(
            dimension_semantics=("arbitrary",), vmem_limit_bytes=VMEM_LIMIT),
        name=name,
    )(*args)


def _mix_kernel(h_ref, pos_ref, pre_g_ref, w_in_ref, b_in_ref, invf_ref, sgn_ref,
                dintra_ref, qdec_ref, kdec_ref, cdec_ref, sink_ref, w_out_ref, post_g_ref,
                o_ref, z_ref, mixin_ref, state_ref, kprev_ref, vprev_ref):
    seq_tile = pl.program_id(1)

    @pl.when(seq_tile == 0)
    def _():
        state_ref[...] = jnp.zeros_like(state_ref)
        kprev_ref[...] = jnp.zeros_like(kprev_ref)
        vprev_ref[...] = jnp.zeros_like(vprev_ref)

    h = h_ref[...]
    u = _rms(h, pre_g_ref[...]).astype(BF16)
    z_ref[...] = jnp.dot(u, w_in_ref[...], preferred_element_type=F32) + b_in_ref[...]

    ang = pos_ref[...].astype(F32) * invf_ref[...]
    cos = jnp.cos(ang)
    sin = jnp.sin(ang) * sgn_ref[...]
    lane = lax.broadcasted_iota(jnp.int32, (CHUNK, LANES), 1)
    lo_half = (lane % RET_QK_DIM) < (RET_QK_DIM // 2)
    lo_head = lane < RET_QK_DIM

    def rotary(x, c, s):
        swapped = jnp.where(lo_half, pltpu.roll(x, LANES - RET_QK_DIM // 2, 1),
                            pltpu.roll(x, RET_QK_DIM // 2, 1))
        return x * c + swapped * s

    qi = lax.broadcasted_iota(jnp.int32, (CHUNK, 2 * CHUNK), 0)
    kj = lax.broadcasted_iota(jnp.int32, (CHUNK, 2 * CHUNK), 1)
    band = (kj > qi) & (kj <= qi + CHUNK)

    n_chunks = h_ref.shape[0] // CHUNK
    for c in range(n_chunks):
        rows = pl.ds(c * CHUNK, CHUNK)
        cos_c = cos[c * CHUNK:(c + 1) * CHUNK]
        sin_c = sin[c * CHUNK:(c + 1) * CHUNK]

        for col in range(RET_QK_W // LANES):
            q2 = rotary(z_ref[rows, pl.ds(OFF_RQ + col * LANES, LANES)], cos_c, sin_c)
            k2 = rotary(z_ref[rows, pl.ds(OFF_RK + col * LANES, LANES)], cos_c, sin_c) * (RET_QK_DIM ** -0.5)
            q2b = q2.astype(BF16)
            q2d = (q2 * qdec_ref[col]).astype(BF16)
            k2d = k2 * kdec_ref[col]
            for sub in range(2):
                head = 2 * col + sub
                keep = lo_head if sub == 0 else ~lo_head
                kh = jnp.where(keep, k2, 0.0).astype(BF16)
                khd = jnp.where(keep, k2d, 0.0).astype(BF16)
                v = z_ref[rows, pl.ds(OFF_RV + head * RET_V_DIM, RET_V_DIM)].astype(BF16)
                s = lax.dot_general(q2b, kh, (((1,), (1,)), ((), ())), preferred_element_type=F32)
                s = (s * dintra_ref[head]).astype(BF16)
                st = state_ref[head]
                y = (jnp.dot(s, v, preferred_element_type=F32)
                     + jnp.dot(q2d, st.astype(BF16), preferred_element_type=F32))
                kv = lax.dot_general(khd, v, (((0,), (0,)), ((), ())), preferred_element_type=F32)
                state_ref[head] = cdec_ref[head] * st + kv
                mu = jnp.mean(y, axis=-1, keepdims=True)
                yc = y - mu
                var = jnp.mean(yc * yc, axis=-1, keepdims=True)
                yn = yc * lax.rsqrt(var + EPS)
                gte = z_ref[rows, pl.ds(OFF_RG + head * RET_V_DIM, RET_V_DIM)]
                mixin_ref[rows, pl.ds(head * RET_V_DIM, RET_V_DIM)] = (
                    gte * jax.nn.sigmoid(gte) * yn).astype(BF16)

        first = jnp.logical_and(seq_tile == 0, c == 0)
        key_lo = jnp.where(first, CHUNK, 0)
        valid = band & (kj >= key_lo)
        k_own = z_ref[rows, pl.ds(OFF_SK, SWA_KV_W)]
        v_own = z_ref[rows, pl.ds(OFF_SV, SWA_KV_W)]
        if c == 0:
            k_prev, v_prev = kprev_ref[...], vprev_ref[...]
        else:
            prev = pl.ds((c - 1) * CHUNK, CHUNK)
            k_prev, v_prev = z_ref[prev, pl.ds(OFF_SK, SWA_KV_W)], z_ref[prev, pl.ds(OFF_SV, SWA_KV_W)]
        kw = jnp.concatenate([k_prev, k_own], axis=0)
        vw = jnp.concatenate([v_prev, v_own], axis=0)
        kw_sw = pltpu.roll(kw, SWA_HEAD_DIM, 1)
        vw_sw = pltpu.roll(vw, SWA_HEAD_DIM, 1)
        lo256 = lax.broadcasted_iota(jnp.int32, (2 * CHUNK, LANES), 1) < SWA_HEAD_DIM
        for j in range(SWA_KV_HEADS):
            k_a, k_b = (kw, kw_sw) if j == 0 else (kw_sw, kw)
            v_a, v_b = (vw, vw_sw) if j == 0 else (vw_sw, vw)
            k_lo = jnp.where(lo256, k_a, 0.0).astype(BF16)
            k_hi = jnp.where(lo256, 0.0, k_b).astype(BF16)
            v_lo = jnp.where(lo256, v_a, 0.0).astype(BF16)
            v_hi = jnp.where(lo256, 0.0, v_b).astype(BF16)
            for col in range(2):
                qcol = OFF_SQ + (2 * j + col) * LANES
                q2 = (z_ref[rows, pl.ds(qcol, LANES)] * (SWA_HEAD_DIM ** -0.5)).astype(BF16)
                outs, invs = [], []
                for sub, (kk, vv) in enumerate(((k_lo, v_lo), (k_hi, v_hi))):
                    head = 4 * j + 2 * col + sub
                    sc = lax.dot_general(q2, kk, (((1,), (1,)), ((), ())), preferred_element_type=F32)
                    sc = jnp.where(valid, sc, NEG_INF)
                    sink = sink_ref[head]
                    m = jnp.maximum(jnp.max(sc, axis=-1, keepdims=True), sink)
                    e = jnp.exp(sc - m)
                    den = jnp.sum(e, axis=-1, keepdims=True) + jnp.exp(sink - m)
                    outs.append(jnp.dot(e.astype(BF16), vv, preferred_element_type=F32))
                    invs.append(1.0 / den)
                o2 = (outs[0] + outs[1]) * jnp.where(lo_head, invs[0], invs[1])
                mixin_ref[rows, pl.ds(RET_V_W + (2 * j + col) * LANES, LANES)] = o2.astype(BF16)

    last = pl.ds((n_chunks - 1) * CHUNK, CHUNK)
    kprev_ref[...] = z_ref[last, pl.ds(OFF_SK, SWA_KV_W)]
    vprev_ref[...] = z_ref[last, pl.ds(OFF_SV, SWA_KV_W)]

    mix = jnp.dot(mixin_ref[...], w_out_ref[...], preferred_element_type=F32)
    o_ref[...] = h + _rms(mix, post_g_ref[...])


def _mix_call(h, pos, pre_g, w_in, b_in, tables, sinks, w_out, post_g, *, batch, seq):
    t, d = h.shape
    n_seq = seq // TM_MIX
    invf, sgn, dintra, qdec, kdec, cdec = tables
    tile = pl.BlockSpec((TM_MIX, d), lambda b, s: (b * n_seq + s, 0))
    in_specs = [
        tile,
        pl.BlockSpec((TM_MIX, 1), lambda b, s: (b * n_seq + s, 0)),
        _const_spec((1, d)),
        _const_spec((d, IN_W)),
        _const_spec((1, IN_W)),
        _const_spec((1, LANES)),
        _const_spec((1, LANES)),
        _const_spec(dintra.shape),
        _const_spec(qdec.shape),
        _const_spec(kdec.shape),
        _const_spec(cdec.shape),
        pl.BlockSpec(memory_space=pltpu.SMEM),
        _const_spec((d, d)),
        _const_spec((1, d)),
    ]
    return pl.pallas_call(
        _mix_kernel,
        grid=(batch, n_seq),
        in_specs=in_specs,
        out_specs=tile,
        out_shape=jax.ShapeDtypeStruct((t, d), F32),
        scratch_shapes=[
            pltpu.VMEM((TM_MIX, IN_W), F32),
            pltpu.VMEM((TM_MIX, d), BF16),
            pltpu.VMEM((RET_HEADS, CHUNK, RET_V_DIM), F32),
            pltpu.VMEM((CHUNK, SWA_KV_W), F32),
            pltpu.VMEM((CHUNK, SWA_KV_W), F32),
        ],
        compiler_params=pltpu.CompilerParams(
            dimension_semantics=("arbitrary", "arbitrary"), vmem_limit_bytes=VMEM_LIMIT),
        name="mix",
    )(h, pos, pre_g, w_in, b_in, invf, sgn, dintra, qdec, kdec, cdec, sinks, w_out, post_g)


def _retention_tables():
    half = RET_QK_DIM // 2
    inv_freq = ROPE_BASE ** (-jnp.arange(half, dtype=F32) / half)
    invf = jnp.tile(inv_freq, LANES // half)[None, :]
    sgn = jnp.tile(jnp.concatenate([-jnp.ones(half, F32), jnp.ones(half, F32)]), LANES // RET_QK_DIM)[None, :]
    log_gamma = jnp.log(1.0 - 2.0 ** (-5.0 - jnp.arange(RET_HEADS, dtype=F32)))
    idx = jnp.arange(CHUNK, dtype=F32)
    diff = idx[:, None] - idx[None, :]
    causal = diff >= 0
    dintra = jnp.where(causal[None], jnp.exp(log_gamma[:, None, None] * jnp.where(causal, diff, 0.0)[None]), 0.0)
    q_decay = jnp.exp(log_gamma[:, None] * (idx[None, :] + 1.0))
    k_decay = jnp.exp(log_gamma[:, None] * (CHUNK - 1.0 - idx[None, :]))
    chunk_decay = jnp.exp(log_gamma * CHUNK)

    def per_column(dec):
        x = jnp.repeat(dec[:, :, None], RET_QK_DIM, axis=2)
        x = x.reshape(RET_HEADS // 2, 2, CHUNK, RET_QK_DIM)
        return jnp.transpose(x, (0, 2, 1, 3)).reshape(RET_HEADS // 2, CHUNK, LANES)

    cdec = jnp.broadcast_to(chunk_decay[:, None, None], (RET_HEADS, CHUNK, RET_V_DIM))
    return invf, sgn, dintra, per_column(q_decay), per_column(k_decay), cdec


def _chunked_cols(w):
    d = w.shape[0]
    return jnp.transpose(w.reshape(d, D_FF // FF_CHUNK, FF_CHUNK), (1, 0, 2)).astype(BF16)


def _chunked_rows(w):
    return w.reshape(D_FF // FF_CHUNK, FF_CHUNK, w.shape[1]).astype(BF16)


def kernel(x, p, positions, ffn1_pre_g, ffn1_post_g, ffn1_w_gate, ffn1_w_up, ffn1_w_down, mix_pre_g, mix_post_g, w_in, b_in, swa_sinks, w_out, ffn2_pre_g, ffn2_post_g, ffn2_w_gate, ffn2_w_up, ffn2_w_down, ple_w_proj, ple_w_gate, ple_norm_g):
    batch, seq, d = x.shape
    depth = p.shape[0]
    assert d == D_MODEL and seq % TM_MIX == 0 and (batch * seq) % TM_FFN == 0
    t = batch * seq
    h = x.reshape(t, d)
    pos = positions.reshape(t, 1)
    tables = _retention_tables()
    for i in range(depth):
        h = _ffn_call(h, ffn1_pre_g[i][None], ffn1_post_g[i][None],
                      _chunked_cols(ffn1_w_gate[i]), _chunked_cols(ffn1_w_up[i]), _chunked_rows(ffn1_w_down[i]),
                      name="ffn1")
        h = _mix_call(h, pos, mix_pre_g[i][None], w_in[i].astype(BF16), b_in[i][None], tables,
                      swa_sinks[i], w_out[i].astype(BF16), mix_post_g[i][None], batch=batch, seq=seq)
        h = _ffn_call(h, ffn2_pre_g[i][None], ffn2_post_g[i][None],
                      _chunked_cols(ffn2_w_gate[i]), _chunked_cols(ffn2_w_up[i]), _chunked_rows(ffn2_w_down[i]),
                      ple=(p[i].reshape(t, PLE_DIM), ple_w_gate[i].astype(BF16), ple_w_proj[i].astype(BF16),
                           ple_norm_g[i][None]),
                      name="ffn2_ple")
    return h.reshape(batch, seq, d)
```

```python
import functools

import jax
import jax.numpy as jnp
from jax import lax
from jax.experimental import pallas as pl
from jax.experimental.pallas import tpu as pltpu

D_MODEL = 1024
PLE_DIM = 256
D_FF = 2816
RET_HEADS = 4
RET_QK_DIM = 64
RET_V_DIM = 128
CHUNK = 128
SWA_Q_HEADS = 8
SWA_KV_HEADS = 2
SWA_HEAD_DIM = 64
ROPE_BASE = 10000.0
EPS = 1e-6
NEG_INF = -1e30

RET_QK_W = RET_HEADS * RET_QK_DIM
RET_V_W = RET_HEADS * RET_V_DIM
SWA_Q_W = SWA_Q_HEADS * SWA_HEAD_DIM
SWA_KV_W = SWA_KV_HEADS * SWA_HEAD_DIM
IN_W = 2 * RET_QK_W + 2 * RET_V_W + SWA_Q_W + 2 * SWA_KV_W
OFF_RQ = 0
OFF_RK = OFF_RQ + RET_QK_W
OFF_RV = OFF_RK + RET_QK_W
OFF_RG = OFF_RV + RET_V_W
OFF_SQ = OFF_RG + RET_V_W
OFF_SK = OFF_SQ + SWA_Q_W
OFF_SV = OFF_SK + SWA_KV_W
Z_RQ = 0
Z_RQD = Z_RQ + RET_QK_W
Z_RK = Z_RQD + RET_QK_W
Z_RKD = Z_RK + RET_QK_W
Z_RV = Z_RKD + RET_QK_W
Z_SQ = Z_RV + RET_V_W
Z_SK = Z_SQ + SWA_Q_W
Z_SKX = Z_SK + SWA_KV_W
Z_SV = Z_SKX + SWA_KV_W
Z_SVX = Z_SV + SWA_KV_W
Z_W = Z_SVX + SWA_KV_W

LANES = 128
FF_CHUNK = 256
TM = 512
TRIG_ROWS = 64
VMEM_LIMIT = 60 * 1024 * 1024

BF16 = jnp.bfloat16
F32 = jnp.float32


def _rms(x, g):
    return x * lax.rsqrt(jnp.mean(x * x, axis=-1, keepdims=True) + EPS) * g


def _const_spec(shape):
    nd = len(shape)
    return pl.BlockSpec(shape, lambda *_: (0,) * nd, pipeline_mode=pl.Buffered(1))


def _gate_up(a, wg_ref, wu_ref, c):
    cols = slice(c * FF_CHUNK, (c + 1) * FF_CHUNK)
    g = jnp.dot(a, wg_ref[:, cols], preferred_element_type=F32)
    u = jnp.dot(a, wu_ref[:, cols], preferred_element_type=F32)
    return (g * jax.nn.sigmoid(g) * u).astype(BF16)


def _down(acc_ref, hm, wd_ref, c):
    d = jnp.dot(hm, wd_ref[c * FF_CHUNK:(c + 1) * FF_CHUNK, :], preferred_element_type=F32)
    if c == 0:
        acc_ref[...] = d
    else:
        acc_ref[...] += d


def _ffn1_proj_kernel(x_ref, pos_ref, pre_g_ref, post_g_ref, wg_ref, wu_ref, wd_ref,
                      mix_g_ref, w_in_ref, b_in_ref, invf_ref, sgn_ref, qdec_ref, kdec_ref,
                      h_ref, z_ref, rg_ref, acc_ref, u_ref, trig_ref):
    @pl.when(pl.program_id(0) == 0)
    def _():
        u_ref[...] = jnp.zeros_like(u_ref)

    u = u_ref[...]

    def proj(off, width):
        cols = slice(off, off + width)
        return jnp.dot(u, w_in_ref[:, cols], preferred_element_type=F32) + b_in_ref[:, cols]

    def store_trig(piece):
        rows = slice(piece * TRIG_ROWS, (piece + 1) * TRIG_ROWS)
        ang = pos_ref[rows, :].astype(F32) * invf_ref[...]
        trig_ref[0, rows, :] = jnp.cos(ang)
        trig_ref[1, rows, :] = jnp.sin(ang) * sgn_ref[...]

    def store_rotary():
        rqk = proj(OFF_RQ, 2 * RET_QK_W)
        cos, sin = trig_ref[0], trig_ref[1]
        lane = lax.broadcasted_iota(jnp.int32, (TM, LANES), 1)
        lo_half = (lane % RET_QK_DIM) < (RET_QK_DIM // 2)

        def rotary(v):
            swapped = jnp.where(lo_half, pltpu.roll(v, LANES - RET_QK_DIM // 2, 1),
                                pltpu.roll(v, RET_QK_DIM // 2, 1))
            return v * cos + swapped * sin

        for col in range(RET_QK_W // LANES):
            q = rotary(rqk[:, col * LANES:(col + 1) * LANES])
            k = rotary(rqk[:, RET_QK_W + col * LANES:RET_QK_W + (col + 1) * LANES]) * (RET_QK_DIM ** -0.5)
            cols = slice(col * LANES, (col + 1) * LANES)
            z_ref[:, Z_RQ + col * LANES:Z_RQ + (col + 1) * LANES] = q.astype(BF16)
            z_ref[:, Z_RK + col * LANES:Z_RK + (col + 1) * LANES] = k.astype(BF16)
            z_ref[:, Z_RQD + col * LANES:Z_RQD + (col + 1) * LANES] = (q * qdec_ref[:, cols]).astype(BF16)
            z_ref[:, Z_RKD + col * LANES:Z_RKD + (col + 1) * LANES] = (k * kdec_ref[:, cols]).astype(BF16)

    def store_swa_kv():
        skv = proj(OFF_SK, 2 * SWA_KV_W)
        z_ref[:, Z_SK:Z_SK + SWA_KV_W] = skv[:, :SWA_KV_W].astype(BF16)
        z_ref[:, Z_SV:Z_SV + SWA_KV_W] = skv[:, SWA_KV_W:].astype(BF16)
        z_ref[:, Z_SKX:Z_SKX + SWA_KV_W] = pltpu.roll(skv[:, :SWA_KV_W], SWA_HEAD_DIM, 1).astype(BF16)
        z_ref[:, Z_SVX:Z_SVX + SWA_KV_W] = pltpu.roll(skv[:, SWA_KV_W:], SWA_HEAD_DIM, 1).astype(BF16)

    def store_swa_q():
        z_ref[:, Z_SQ:Z_SQ + SWA_Q_W] = (proj(OFF_SQ, SWA_Q_W) * (SWA_HEAD_DIM ** -0.5)).astype(BF16)

    def store_ret_v():
        z_ref[:, Z_RV:Z_RV + RET_V_W] = proj(OFF_RV, RET_V_W).astype(BF16)

    n_trig = TM // TRIG_ROWS
    n_ffn = D_FF // FF_CHUNK
    assert n_trig + 1 < n_ffn
    between = {1: store_swa_kv, 3: store_swa_q, n_trig: store_rotary}

    rg_ref[...] = proj(OFF_RG, RET_V_W)
    x = x_ref[...]
    a = _rms(x, pre_g_ref[...]).astype(BF16)
    for c in range(n_ffn):
        hm = _gate_up(a, wg_ref, wu_ref, c)
        if c < n_trig:
            store_trig(c)
        if c in between:
            between[c]()
        _down(acc_ref, hm, wd_ref, c)
    store_ret_v()
    h = x + 0.5 * _rms(acc_ref[...], post_g_ref[...])
    h_ref[...] = h
    u_ref[...] = _rms(h, mix_g_ref[...]).astype(BF16)


def _ffn1_proj_call(x, pos, pre_g, post_g, wg, wu, wd, mix_g, w_in, b_in, invf, sgn, qdec, kdec):
    t, d = x.shape
    n_tiles = t // TM
    cur = lambda w: pl.BlockSpec((TM, w), lambda i: (jnp.minimum(i, n_tiles - 1), 0))
    prev = lambda w: pl.BlockSpec((TM, w), lambda i: (jnp.maximum(i - 1, 0), 0))
    in_specs = [cur(d), prev(1), _const_spec((1, d)), _const_spec((1, d)),
                _const_spec((d, D_FF)), _const_spec((d, D_FF)), _const_spec((D_FF, d)),
                _const_spec((1, d)), _const_spec((d, IN_W)), _const_spec((1, IN_W)),
                _const_spec((1, LANES)), _const_spec((1, LANES)),
                _const_spec((TM, RET_QK_W)), _const_spec((TM, RET_QK_W))]
    return pl.pallas_call(
        _ffn1_proj_kernel,
        grid=(n_tiles + 1,),
        in_specs=in_specs,
        out_specs=[cur(d), prev(Z_W), prev(RET_V_W)],
        out_shape=[jax.ShapeDtypeStruct((t, d), F32), jax.ShapeDtypeStruct((t, Z_W), BF16),
                   jax.ShapeDtypeStruct((t, RET_V_W), F32)],
        scratch_shapes=[pltpu.VMEM((TM, d), F32),
                        pltpu.VMEM((TM, d), BF16),
                        pltpu.VMEM((2, TM, LANES), F32)],
        compiler_params=pltpu.CompilerParams(
            dimension_semantics=("arbitrary",), vmem_limit_bytes=VMEM_LIMIT),
        name="ffn1_proj",
    )(x, pos, pre_g, post_g, wg, wu, wd, mix_g, w_in, b_in, invf, sgn, qdec, kdec)


class _Attention:
    def __init__(self, z_ref, rg_ref, dintra_ref, cdec_ref, sink_ref, mixin_ref, state_ref, kprev_ref, vprev_ref,
                 seq_tile):
        self.z_ref, self.rg_ref, self.dintra_ref, self.cdec_ref = z_ref, rg_ref, dintra_ref, cdec_ref
        self.sink_ref, self.mixin_ref, self.state_ref = sink_ref, mixin_ref, state_ref
        self.kprev_ref, self.vprev_ref, self.seq_tile = kprev_ref, vprev_ref, seq_tile
        lane = lax.broadcasted_iota(jnp.int32, (CHUNK, LANES), 1)
        self.lo_head = lane < RET_QK_DIM
        self.lo256 = lax.broadcasted_iota(jnp.int32, (2 * CHUNK, LANES), 1) < SWA_HEAD_DIM
        qi = lax.broadcasted_iota(jnp.int32, (CHUNK, 2 * CHUNK), 0)
        self.kj = lax.broadcasted_iota(jnp.int32, (CHUNK, 2 * CHUNK), 1)
        self.band = (self.kj > qi) & (self.kj <= qi + CHUNK)

    def scores(self, c):
        z_ref = self.z_ref
        rows = slice(c * CHUNK, (c + 1) * CHUNK)
        zero = jnp.zeros((), BF16)
        nt = (((1,), (1,)), ((), ()))
        ret = []
        for col in range(RET_QK_W // LANES):
            q2 = z_ref[rows, Z_RQ + col * LANES:Z_RQ + (col + 1) * LANES]
            k2 = z_ref[rows, Z_RK + col * LANES:Z_RK + (col + 1) * LANES]
            k2d = z_ref[rows, Z_RKD + col * LANES:Z_RKD + (col + 1) * LANES]
            for sub in range(2):
                head = 2 * col + sub
                keep = self.lo_head if sub == 0 else ~self.lo_head
                kh = jnp.where(keep, k2, zero)
                khd = jnp.where(keep, k2d, zero)
                v = z_ref[rows, Z_RV + head * RET_V_DIM:Z_RV + (head + 1) * RET_V_DIM]
                s = lax.dot_general(q2, kh, nt, preferred_element_type=F32)
                kv = lax.dot_general(khd, v, (((0,), (0,)), ((), ())), preferred_element_type=F32)
                ret.append((s, kv, v))

        def window(off, prev_ref, lane_off):
            own = z_ref[rows, off:off + SWA_KV_W]
            if c == 0:
                prev = prev_ref[:, lane_off:lane_off + SWA_KV_W]
            else:
                prev = z_ref[(c - 1) * CHUNK:c * CHUNK, off:off + SWA_KV_W]
            return jnp.concatenate([prev, own], axis=0)

        kw, kwx = window(Z_SK, self.kprev_ref, 0), window(Z_SKX, self.kprev_ref, SWA_KV_W)
        vw, vwx = window(Z_SV, self.vprev_ref, 0), window(Z_SVX, self.vprev_ref, SWA_KV_W)
        swa = []
        for j in range(SWA_KV_HEADS):
            k_a, k_b = (kw, kwx) if j == 0 else (kwx, kw)
            v_a, v_b = (vw, vwx) if j == 0 else (vwx, vw)
            ks = (jnp.where(self.lo256, k_a, zero), jnp.where(self.lo256, zero, k_b))
            vs = (jnp.where(self.lo256, v_a, zero), jnp.where(self.lo256, zero, v_b))
            for col in range(2):
                qcol = Z_SQ + (2 * j + col) * LANES
                q2 = z_ref[rows, qcol:qcol + LANES]
                sc = [lax.dot_general(q2, kk, nt, preferred_element_type=F32) for kk in ks]
                swa.append((sc, vs))
        return ret, swa

    def finish(self, c, staged):
        ret, swa = staged
        rows = slice(c * CHUNK, (c + 1) * CHUNK)
        for head, (s, kv, v) in enumerate(ret):
            col = head // 2
            q2d = self.z_ref[rows, Z_RQD + col * LANES:Z_RQD + (col + 1) * LANES]
            s = (s * self.dintra_ref[head]).astype(BF16)
            st = self.state_ref[head]
            y = (jnp.dot(s, v, preferred_element_type=F32)
                 + jnp.dot(q2d, st.astype(BF16), preferred_element_type=F32))
            self.state_ref[head] = self.cdec_ref[head] * st + kv
            mu = jnp.mean(y, axis=-1, keepdims=True)
            yc = y - mu
            var = jnp.mean(yc * yc, axis=-1, keepdims=True)
            yn = yc * lax.rsqrt(var + EPS)
            gte = self.rg_ref[rows, head * RET_V_DIM:(head + 1) * RET_V_DIM]
            self.mixin_ref[rows, head * RET_V_DIM:(head + 1) * RET_V_DIM] = (
                gte * jax.nn.sigmoid(gte) * yn).astype(BF16)

        first = jnp.logical_and(self.seq_tile == 0, c == 0)
        key_lo = jnp.where(first, CHUNK, 0)
        valid = self.band & (self.kj >= key_lo)
        for pair, (scs, vs) in enumerate(swa):
            outs, invs = [], []
            for sub, (sc, vv) in enumerate(zip(scs, vs)):
                sc = jnp.where(valid, sc, NEG_INF)
                sink = self.sink_ref[2 * pair + sub]
                m = jnp.maximum(jnp.max(sc, axis=-1, keepdims=True), sink)
                e = jnp.exp(sc - m)
                den = jnp.sum(e, axis=-1, keepdims=True) + jnp.exp(sink - m)
                outs.append(jnp.dot(e.astype(BF16), vv, preferred_element_type=F32))
                invs.append(1.0 / den)
            o2 = (outs[0] + outs[1]) * jnp.where(self.lo_head, invs[0], invs[1])
            ocol = RET_V_W + pair * LANES
            self.mixin_ref[rows, ocol:ocol + LANES] = o2.astype(BF16)

    def carry(self):
        last = slice(TM - CHUNK, TM)
        z_ref = self.z_ref
        self.kprev_ref[:, :SWA_KV_W] = z_ref[last, Z_SK:Z_SK + SWA_KV_W]
        self.kprev_ref[:, SWA_KV_W:] = z_ref[last, Z_SKX:Z_SKX + SWA_KV_W]
        self.vprev_ref[:, :SWA_KV_W] = z_ref[last, Z_SV:Z_SV + SWA_KV_W]
        self.vprev_ref[:, SWA_KV_W:] = z_ref[last, Z_SVX:Z_SVX + SWA_KV_W]


def _mix_ffn2_kernel(h1_ref, z_ref, rg_ref, dintra_ref, cdec_ref, sink_ref, w_out_ref, mix_g_ref,
                     pre_g_ref, post_g_ref, wg_ref, wu_ref, wd_ref, p_ref, wpg_ref, wpp_ref, ple_g_ref,
                     o_ref, h2_ref, acc_ref, mixin_ref, state_ref, kprev_ref, vprev_ref, *, tiles_per_seq):
    step = pl.program_id(0)
    n_tiles = pl.num_programs(0) - 1
    seq_tile = jnp.minimum(step, n_tiles - 1) % tiles_per_seq

    @pl.when(step == 0)
    def _():
        h2_ref[...] = jnp.zeros_like(h2_ref)

    @pl.when(seq_tile == 0)
    def _():
        state_ref[...] = jnp.zeros_like(state_ref)
        kprev_ref[...] = jnp.zeros_like(kprev_ref)
        vprev_ref[...] = jnp.zeros_like(vprev_ref)

    att = _Attention(z_ref, rg_ref, dintra_ref, cdec_ref, sink_ref, mixin_ref, state_ref, kprev_ref, vprev_ref,
                     seq_tile)
    n_att = TM // CHUNK
    n_ffn = D_FF // FF_CHUNK
    assert n_att < n_ffn

    x = h2_ref[...]
    staged = att.scores(0)
    a = _rms(x, pre_g_ref[...]).astype(BF16)
    for c in range(n_ffn):
        hm = _gate_up(a, wg_ref, wu_ref, c)
        if c < n_att:
            att.finish(c, staged)
            if c + 1 < n_att:
                staged = att.scores(c + 1)
        elif c == n_att:
            att.carry()
            proj = jnp.dot(p_ref[...].astype(BF16), wpp_ref[...], preferred_element_type=F32)
        _down(acc_ref, hm, wd_ref, c)

    mix = jnp.dot(mixin_ref[...], w_out_ref[...], preferred_element_type=F32)
    h = x + 0.5 * _rms(acc_ref[...], post_g_ref[...])
    gate = jax.nn.sigmoid(jnp.dot(h.astype(BF16), wpg_ref[...], preferred_element_type=F32))
    h2_ref[...] = h1_ref[...] + _rms(mix, mix_g_ref[...])
    o_ref[...] = h + _rms(gate * proj, ple_g_ref[...])


def _mix_ffn2_call(h1, z, rg, dintra, cdec, sinks, w_out, mix_g, pre_g, post_g, wg, wu, wd,
                   p, wpg, wpp, ple_g, *, seq):
    t, d = h1.shape
    n_tiles = t // TM
    cur = lambda w: pl.BlockSpec((TM, w), lambda i: (jnp.minimum(i, n_tiles - 1), 0))
    prev = lambda w: pl.BlockSpec((TM, w), lambda i: (jnp.maximum(i - 1, 0), 0))
    in_specs = [cur(d), cur(Z_W), cur(RET_V_W),
                _const_spec(dintra.shape), _const_spec(cdec.shape),
                pl.BlockSpec(memory_space=pltpu.SMEM),
                _const_spec((d, d)), _const_spec((1, d)),
                _const_spec((1, d)), _const_spec((1, d)),
                _const_spec((d, D_FF)), _const_spec((d, D_FF)), _const_spec((D_FF, d)),
                prev(PLE_DIM), _const_spec((d, d)), _const_spec((PLE_DIM, d)), _const_spec((1, d))]
    return pl.pallas_call(
        functools.partial(_mix_ffn2_kernel, tiles_per_seq=seq // TM),
        grid=(n_tiles + 1,),
        in_specs=in_specs,
        out_specs=prev(d),
        out_shape=jax.ShapeDtypeStruct((t, d), F32),
        scratch_shapes=[
            pltpu.VMEM((TM, d), F32),
            pltpu.VMEM((TM, d), F32),
            pltpu.VMEM((TM, d), BF16),
            pltpu.VMEM((RET_HEADS, CHUNK, RET_V_DIM), F32),
            pltpu.VMEM((CHUNK, 2 * SWA_KV_W), BF16),
            pltpu.VMEM((CHUNK, 2 * SWA_KV_W), BF16),
        ],
        compiler_params=pltpu.CompilerParams(
            dimension_semantics=("arbitrary",), vmem_limit_bytes=VMEM_LIMIT),
        name="mix_ffn2",
    )(h1, z, rg, dintra, cdec, sinks, w_out, mix_g, pre_g, post_g, wg, wu, wd, p, wpg, wpp, ple_g)


def _tables():
    half = RET_QK_DIM // 2
    inv_freq = ROPE_BASE ** (-jnp.arange(half, dtype=F32) / half)
    invf = jnp.tile(inv_freq, LANES // half)[None, :]
    sgn = jnp.tile(jnp.concatenate([-jnp.ones(half, F32), jnp.ones(half, F32)]), LANES // RET_QK_DIM)[None, :]
    log_gamma = jnp.log(1.0 - 2.0 ** (-5.0 - jnp.arange(RET_HEADS, dtype=F32)))
    idx = jnp.arange(CHUNK, dtype=F32)
    diff = idx[:, None] - idx[None, :]
    causal = diff >= 0
    dintra = jnp.where(causal[None], jnp.exp(log_gamma[:, None, None] * jnp.where(causal, diff, 0.0)[None]), 0.0)
    q_decay = jnp.exp(log_gamma[:, None] * (idx[None, :] + 1.0))
    k_decay = jnp.exp(log_gamma[:, None] * (CHUNK - 1.0 - idx[None, :]))
    chunk_decay = jnp.exp(log_gamma * CHUNK)

    def per_token(dec):
        x = jnp.repeat(jnp.transpose(dec)[:, :, None], RET_QK_DIM, axis=2).reshape(CHUNK, RET_QK_W)
        return jnp.tile(x, (TM // CHUNK, 1))

    cdec = jnp.broadcast_to(chunk_decay[:, None, None], (RET_HEADS, CHUNK, RET_V_DIM))
    return invf, sgn, dintra, per_token(q_decay), per_token(k_decay), cdec


def kernel(x, p, positions, ffn1_pre_g, ffn1_post_g, ffn1_w_gate, ffn1_w_up, ffn1_w_down, mix_pre_g, mix_post_g, w_in, b_in, swa_sinks, w_out, ffn2_pre_g, ffn2_post_g, ffn2_w_gate, ffn2_w_up, ffn2_w_down, ple_w_proj, ple_w_gate, ple_norm_g):
    batch, seq, d = x.shape
    depth = p.shape[0]
    assert d == D_MODEL and seq % TM == 0
    t = batch * seq
    h = x.reshape(t, d)
    pos = positions.reshape(t, 1)
    invf, sgn, dintra, qdec, kdec, cdec = _tables()
    bf = lambda w: w.astype(BF16)
    for i in range(depth):
        h1, z, rg = _ffn1_proj_call(
            h, pos, ffn1_pre_g[i][None], ffn1_post_g[i][None],
            bf(ffn1_w_gate[i]), bf(ffn1_w_up[i]), bf(ffn1_w_down[i]),
            mix_pre_g[i][None], bf(w_in[i]), b_in[i][None], invf, sgn, qdec, kdec)
        h = _mix_ffn2_call(
            h1, z, rg, dintra, cdec, swa_sinks[i], bf(w_out[i]), mix_post_g[i][None],
            ffn2_pre_g[i][None], ffn2_post_g[i][None],
            bf(ffn2_w_gate[i]), bf(ffn2_w_up[i]), bf(ffn2_w_down[i]),
            p[i].reshape(t, PLE_DIM), bf(ple_w_gate[i]), bf(ple_w_proj[i]), ple_norm_g[i][None], seq=seq)
    return h.reshape(batch, seq, d)
```

```python
import functools

import jax
import jax.numpy as jnp
from jax import lax
from jax.experimental import pallas as pl
from jax.experimental.pallas import tpu as pltpu

D_MODEL = 1024
PLE_DIM = 256
D_FF = 2816
RET_HEADS = 4
RET_QK_DIM = 64
RET_V_DIM = 128
CHUNK = 128
SWA_Q_HEADS = 8
SWA_KV_HEADS = 2
SWA_HEAD_DIM = 64
ROPE_BASE = 10000.0
EPS = 1e-6
NEG_INF = -1e30

RET_QK_W = RET_HEADS * RET_QK_DIM
RET_V_W = RET_HEADS * RET_V_DIM
SWA_Q_W = SWA_Q_HEADS * SWA_HEAD_DIM
SWA_KV_W = SWA_KV_HEADS * SWA_HEAD_DIM
IN_W = 2 * RET_QK_W + 2 * RET_V_W + SWA_Q_W + 2 * SWA_KV_W
OFF_RQ = 0
OFF_RK = OFF_RQ + RET_QK_W
OFF_RV = OFF_RK + RET_QK_W
OFF_RG = OFF_RV + RET_V_W
OFF_SQ = OFF_RG + RET_V_W
OFF_SK = OFF_SQ + SWA_Q_W
OFF_SV = OFF_SK + SWA_KV_W
Z_RQ = 0
Z_RQD = Z_RQ + RET_QK_W
Z_RK = Z_RQD + RET_QK_W
Z_RKD = Z_RK + RET_QK_W
Z_RV = Z_RKD + RET_QK_W
Z_SQ = Z_RV + RET_V_W
Z_SK = Z_SQ + SWA_Q_W
Z_SKX = Z_SK + SWA_KV_W
Z_SV = Z_SKX + SWA_KV_W
Z_SVX = Z_SV + SWA_KV_W
Z_W = Z_SVX + SWA_KV_W

LANES = 128
FF_CHUNK = 256
TM = 512
TRIG_ROWS = 64
FFN_SPLIT = 5
VMEM_LIMIT = 60 * 1024 * 1024

BF16 = jnp.bfloat16
F32 = jnp.float32


def _rms(x, g):
    return x * lax.rsqrt(jnp.mean(x * x, axis=-1, keepdims=True) + EPS) * g


def _const_spec(shape):
    nd = len(shape)
    return pl.BlockSpec(shape, lambda *_: (0,) * nd, pipeline_mode=pl.Buffered(1))


def _gate_up(a, wg_ref, wu_ref, c):
    cols = slice(c * FF_CHUNK, (c + 1) * FF_CHUNK)
    g = jnp.dot(a, wg_ref[:, cols], preferred_element_type=F32)
    u = jnp.dot(a, wu_ref[:, cols], preferred_element_type=F32)
    return (g * jax.nn.sigmoid(g) * u).astype(BF16)


def _down(acc_ref, hm, wd_ref, c):
    d = jnp.dot(hm, wd_ref[c * FF_CHUNK:(c + 1) * FF_CHUNK, :], preferred_element_type=F32)
    if c == 0:
        acc_ref[...] = d
    else:
        acc_ref[...] += d


def _ffn_slices(acc_ref, a, wg_ref, wu_ref, wd_ref, lo, hi, between):
    hm_next = _gate_up(a, wg_ref, wu_ref, lo)
    for c in range(lo, hi):
        hm = hm_next
        if c in between:
            between[c]()
        if c + 1 < hi:
            hm_next = _gate_up(a, wg_ref, wu_ref, c + 1)
        _down(acc_ref, hm, wd_ref, c)


def _ffn1_proj_kernel(x_ref, pos_ref, pre_g_ref, post_g_ref, wg_ref, wu_ref, wd_ref,
                      mix_g_ref, w_in_ref, b_in_ref, invf_ref, sgn_ref, qdec_ref, kdec_ref,
                      h_ref, z_ref, rg_ref, acc_ref, u_ref, trig_ref):
    @pl.when(pl.program_id(0) == 0)
    def _():
        u_ref[...] = jnp.zeros_like(u_ref)

    u = u_ref[...]

    def store_trig(piece):
        rows = slice(piece * TRIG_ROWS, (piece + 1) * TRIG_ROWS)
        ang = pos_ref[rows, :].astype(F32) * invf_ref[...]
        trig_ref[0, rows, :] = jnp.cos(ang)
        trig_ref[1, rows, :] = jnp.sin(ang) * sgn_ref[...]

    def proj(off, width):
        cols = slice(off, off + width)
        return jnp.dot(u, w_in_ref[:, cols], preferred_element_type=F32) + b_in_ref[:, cols]

    def store_rotary():
        rqk = proj(OFF_RQ, 2 * RET_QK_W)
        cos, sin = trig_ref[0], trig_ref[1]
        lane = lax.broadcasted_iota(jnp.int32, (TM, LANES), 1)
        lo_half = (lane % RET_QK_DIM) < (RET_QK_DIM // 2)

        def rotary(v):
            swapped = jnp.where(lo_half, pltpu.roll(v, LANES - RET_QK_DIM // 2, 1),
                                pltpu.roll(v, RET_QK_DIM // 2, 1))
            return v * cos + swapped * sin

        for col in range(RET_QK_W // LANES):
            q = rotary(rqk[:, col * LANES:(col + 1) * LANES])
            k = rotary(rqk[:, RET_QK_W + col * LANES:RET_QK_W + (col + 1) * LANES]) * (RET_QK_DIM ** -0.5)
            cols = slice(col * LANES, (col + 1) * LANES)
            z_ref[:, Z_RQ + col * LANES:Z_RQ + (col + 1) * LANES] = q.astype(BF16)
            z_ref[:, Z_RK + col * LANES:Z_RK + (col + 1) * LANES] = k.astype(BF16)
            z_ref[:, Z_RQD + col * LANES:Z_RQD + (col + 1) * LANES] = (q * qdec_ref[:, cols]).astype(BF16)
            z_ref[:, Z_RKD + col * LANES:Z_RKD + (col + 1) * LANES] = (k * kdec_ref[:, cols]).astype(BF16)

    def store_swa_kv():
        skv = proj(OFF_SK, 2 * SWA_KV_W)
        z_ref[:, Z_SK:Z_SK + SWA_KV_W] = skv[:, :SWA_KV_W].astype(BF16)
        z_ref[:, Z_SV:Z_SV + SWA_KV_W] = skv[:, SWA_KV_W:].astype(BF16)
        z_ref[:, Z_SKX:Z_SKX + SWA_KV_W] = pltpu.roll(skv[:, :SWA_KV_W], SWA_HEAD_DIM, 1).astype(BF16)
        z_ref[:, Z_SVX:Z_SVX + SWA_KV_W] = pltpu.roll(skv[:, SWA_KV_W:], SWA_HEAD_DIM, 1).astype(BF16)

    def store_swa_q():
        z_ref[:, Z_SQ:Z_SQ + SWA_Q_W] = (proj(OFF_SQ, SWA_Q_W) * (SWA_HEAD_DIM ** -0.5)).astype(BF16)

    def store_ret_v():
        z_ref[:, Z_RV:Z_RV + RET_V_W] = proj(OFF_RV, RET_V_W).astype(BF16)

    n_trig = TM // TRIG_ROWS
    n_ffn = D_FF // FF_CHUNK
    assert n_trig + 1 < n_ffn
    extra = {1: store_swa_kv, 3: store_swa_q, n_trig: store_rotary}

    def after_slice(c):
        if c < n_trig:
            store_trig(c)
        if c in extra:
            extra[c]()

    rg_ref[...] = proj(OFF_RG, RET_V_W)
    x = x_ref[...]
    a = _rms(x, pre_g_ref[...]).astype(BF16)
    _ffn_slices(acc_ref, a, wg_ref, wu_ref, wd_ref, 0, n_ffn,
                {c: functools.partial(after_slice, c) for c in range(n_trig + 1)})
    store_ret_v()
    h = x + 0.5 * _rms(acc_ref[...], post_g_ref[...])
    h_ref[...] = h
    u_ref[...] = _rms(h, mix_g_ref[...]).astype(BF16)


def _ffn1_proj_call(x, pos, pre_g, post_g, wg, wu, wd, mix_g, w_in, b_in, invf, sgn, qdec, kdec):
    t, d = x.shape
    n_tiles = t // TM
    cur = lambda w: pl.BlockSpec((TM, w), lambda i: (jnp.minimum(i, n_tiles - 1), 0))
    prev = lambda w: pl.BlockSpec((TM, w), lambda i: (jnp.maximum(i - 1, 0), 0))
    in_specs = [cur(d), prev(1), _const_spec((1, d)), _const_spec((1, d)),
                _const_spec((d, D_FF)), _const_spec((d, D_FF)), _const_spec((D_FF, d)),
                _const_spec((1, d)), _const_spec((d, IN_W)), _const_spec((1, IN_W)),
                _const_spec((1, LANES)), _const_spec((1, LANES)),
                _const_spec((TM, RET_QK_W)), _const_spec((TM, RET_QK_W))]
    return pl.pallas_call(
        _ffn1_proj_kernel,
        grid=(n_tiles + 1,),
        in_specs=in_specs,
        out_specs=[cur(d), prev(Z_W), prev(RET_V_W)],
        out_shape=[jax.ShapeDtypeStruct((t, d), F32), jax.ShapeDtypeStruct((t, Z_W), BF16),
                   jax.ShapeDtypeStruct((t, RET_V_W), F32)],
        scratch_shapes=[pltpu.VMEM((TM, d), F32),
                        pltpu.VMEM((TM, d), BF16),
                        pltpu.VMEM((2, TM, LANES), F32)],
        compiler_params=pltpu.CompilerParams(
            dimension_semantics=("arbitrary",), vmem_limit_bytes=VMEM_LIMIT),
        name="ffn1_proj",
    )(x, pos, pre_g, post_g, wg, wu, wd, mix_g, w_in, b_in, invf, sgn, qdec, kdec)


class _Attention:
    def __init__(self, z_ref, rg_ref, dintra_ref, cdec_ref, sink_ref, mixin_ref, state_ref, kprev_ref, vprev_ref,
                 seq_tile):
        self.z_ref, self.rg_ref, self.dintra_ref, self.cdec_ref = z_ref, rg_ref, dintra_ref, cdec_ref
        self.sink_ref, self.mixin_ref, self.state_ref = sink_ref, mixin_ref, state_ref
        self.kprev_ref, self.vprev_ref, self.seq_tile = kprev_ref, vprev_ref, seq_tile
        lane = lax.broadcasted_iota(jnp.int32, (CHUNK, LANES), 1)
        self.lo_head = lane < RET_QK_DIM
        self.lo256 = lax.broadcasted_iota(jnp.int32, (2 * CHUNK, LANES), 1) < SWA_HEAD_DIM
        qi = lax.broadcasted_iota(jnp.int32, (CHUNK, 2 * CHUNK), 0)
        self.kj = lax.broadcasted_iota(jnp.int32, (CHUNK, 2 * CHUNK), 1)
        self.band = (self.kj > qi) & (self.kj <= qi + CHUNK)

    def scores(self, c):
        z_ref = self.z_ref
        rows = slice(c * CHUNK, (c + 1) * CHUNK)
        zero = jnp.zeros((), BF16)
        nt = (((1,), (1,)), ((), ()))
        ret = []
        for col in range(RET_QK_W // LANES):
            q2 = z_ref[rows, Z_RQ + col * LANES:Z_RQ + (col + 1) * LANES]
            k2 = z_ref[rows, Z_RK + col * LANES:Z_RK + (col + 1) * LANES]
            k2d = z_ref[rows, Z_RKD + col * LANES:Z_RKD + (col + 1) * LANES]
            for sub in range(2):
                head = 2 * col + sub
                keep = self.lo_head if sub == 0 else ~self.lo_head
                kh = jnp.where(keep, k2, zero)
                khd = jnp.where(keep, k2d, zero)
                v = z_ref[rows, Z_RV + head * RET_V_DIM:Z_RV + (head + 1) * RET_V_DIM]
                s = lax.dot_general(q2, kh, nt, preferred_element_type=F32)
                kv = lax.dot_general(khd, v, (((0,), (0,)), ((), ())), preferred_element_type=F32)
                ret.append((s, kv, v))

        def window(off, prev_ref, lane_off):
            own = z_ref[rows, off:off + SWA_KV_W]
            if c == 0:
                prev = prev_ref[:, lane_off:lane_off + SWA_KV_W]
            else:
                prev = z_ref[(c - 1) * CHUNK:c * CHUNK, off:off + SWA_KV_W]
            return jnp.concatenate([prev, own], axis=0)

        kw, kwx = window(Z_SK, self.kprev_ref, 0), window(Z_SKX, self.kprev_ref, SWA_KV_W)
        vw, vwx = window(Z_SV, self.vprev_ref, 0), window(Z_SVX, self.vprev_ref, SWA_KV_W)
        swa = []
        for j in range(SWA_KV_HEADS):
            k_a, k_b = (kw, kwx) if j == 0 else (kwx, kw)
            v_a, v_b = (vw, vwx) if j == 0 else (vwx, vw)
            ks = (jnp.where(self.lo256, k_a, zero), jnp.where(self.lo256, zero, k_b))
            vs = (jnp.where(self.lo256, v_a, zero), jnp.where(self.lo256, zero, v_b))
            for col in range(2):
                qcol = Z_SQ + (2 * j + col) * LANES
                q2 = z_ref[rows, qcol:qcol + LANES]
                sc = [lax.dot_general(q2, kk, nt, preferred_element_type=F32) for kk in ks]
                swa.append((sc, vs))
        return ret, swa

    def finish(self, c, staged):
        ret, swa = staged
        rows = slice(c * CHUNK, (c + 1) * CHUNK)
        for head, (s, kv, v) in enumerate(ret):
            col = head // 2
            q2d = self.z_ref[rows, Z_RQD + col * LANES:Z_RQD + (col + 1) * LANES]
            s = (s * self.dintra_ref[head]).astype(BF16)
            st = self.state_ref[head]
            y = (jnp.dot(s, v, preferred_element_type=F32)
                 + jnp.dot(q2d, st.astype(BF16), preferred_element_type=F32))
            self.state_ref[head] = self.cdec_ref[head] * st + kv
            mu = jnp.mean(y, axis=-1, keepdims=True)
            yc = y - mu
            var = jnp.mean(yc * yc, axis=-1, keepdims=True)
            yn = yc * lax.rsqrt(var + EPS)
            gte = self.rg_ref[rows, head * RET_V_DIM:(head + 1) * RET_V_DIM]
            self.mixin_ref[rows, head * RET_V_DIM:(head + 1) * RET_V_DIM] = (
                gte * jax.nn.sigmoid(gte) * yn).astype(BF16)

        first = jnp.logical_and(self.seq_tile == 0, c == 0)
        key_lo = jnp.where(first, CHUNK, 0)
        valid = self.band & (self.kj >= key_lo)
        for pair, (scs, vs) in enumerate(swa):
            outs, invs = [], []
            for sub, (sc, vv) in enumerate(zip(scs, vs)):
                sc = jnp.where(valid, sc, NEG_INF)
                sink = self.sink_ref[2 * pair + sub]
                m = jnp.maximum(jnp.max(sc, axis=-1, keepdims=True), sink)
                e = jnp.exp(sc - m)
                den = jnp.sum(e, axis=-1, keepdims=True) + jnp.exp(sink - m)
                outs.append(jnp.dot(e.astype(BF16), vv, preferred_element_type=F32))
                invs.append(1.0 / den)
            o2 = (outs[0] + outs[1]) * jnp.where(self.lo_head, invs[0], invs[1])
            ocol = RET_V_W + pair * LANES
            self.mixin_ref[rows, ocol:ocol + LANES] = o2.astype(BF16)

    def carry(self):
        last = slice(TM - CHUNK, TM)
        z_ref = self.z_ref
        self.kprev_ref[:, :SWA_KV_W] = z_ref[last, Z_SK:Z_SK + SWA_KV_W]
        self.kprev_ref[:, SWA_KV_W:] = z_ref[last, Z_SKX:Z_SKX + SWA_KV_W]
        self.vprev_ref[:, :SWA_KV_W] = z_ref[last, Z_SV:Z_SV + SWA_KV_W]
        self.vprev_ref[:, SWA_KV_W:] = z_ref[last, Z_SVX:Z_SVX + SWA_KV_W]


def _mix_ffn2_kernel(h1_ref, z_ref, rg_ref, dintra_ref, cdec_ref, sink_ref, w_out_ref, mix_g_ref,
                     pre_g_ref, post_g_ref, wg_ref, wu_ref, wd_ref, p_ref, wpg_ref, wpp_ref, ple_g_ref,
                     o_ref, h2_ref, a_ref, acc_ref, mixin_ref, state_ref, kprev_ref, vprev_ref, *, tiles_per_seq):
    step = pl.program_id(0)
    n_tiles = pl.num_programs(0) - 1
    seq_tile = jnp.minimum(step, n_tiles - 1) % tiles_per_seq

    @pl.when(step == 0)
    def _():
        h2_ref[...] = jnp.zeros_like(h2_ref)
        a_ref[...] = jnp.zeros_like(a_ref)
        acc_ref[...] = jnp.zeros_like(acc_ref)

    @pl.when(seq_tile == 0)
    def _():
        state_ref[...] = jnp.zeros_like(state_ref)
        kprev_ref[...] = jnp.zeros_like(kprev_ref)
        vprev_ref[...] = jnp.zeros_like(vprev_ref)

    att = _Attention(z_ref, rg_ref, dintra_ref, cdec_ref, sink_ref, mixin_ref, state_ref, kprev_ref, vprev_ref,
                     seq_tile)
    n_att = TM // CHUNK
    n_ffn = D_FF // FF_CHUNK
    assert n_att < n_ffn - FFN_SPLIT
    staged = [att.scores(0)]
    side = {}

    def attention_chunk(k):
        att.finish(k, staged[0])
        if k + 1 < n_att:
            staged[0] = att.scores(k + 1)

    def after_attention():
        att.carry()
        side["proj"] = jnp.dot(p_ref[...].astype(BF16), wpp_ref[...], preferred_element_type=F32)

    between = {FFN_SPLIT + k: functools.partial(attention_chunk, k) for k in range(n_att)}
    between[FFN_SPLIT + n_att] = after_attention
    _ffn_slices(acc_ref, a_ref[...], wg_ref, wu_ref, wd_ref, FFN_SPLIT, n_ffn, between)

    mix = jnp.dot(mixin_ref[...], w_out_ref[...], preferred_element_type=F32)
    h = h2_ref[...] + 0.5 * _rms(acc_ref[...], post_g_ref[...])
    gate = jax.nn.sigmoid(jnp.dot(h.astype(BF16), wpg_ref[...], preferred_element_type=F32))
    h2 = h1_ref[...] + _rms(mix, mix_g_ref[...])
    h2_ref[...] = h2
    a = _rms(h2, pre_g_ref[...]).astype(BF16)
    a_ref[...] = a

    def store_out():
        o_ref[...] = h + _rms(gate * side["proj"], ple_g_ref[...])

    _ffn_slices(acc_ref, a, wg_ref, wu_ref, wd_ref, 0, FFN_SPLIT, {0: store_out})


def _mix_ffn2_call(h1, z, rg, dintra, cdec, sinks, w_out, mix_g, pre_g, post_g, wg, wu, wd,
                   p, wpg, wpp, ple_g, *, seq):
    t, d = h1.shape
    n_tiles = t // TM
    cur = lambda w: pl.BlockSpec((TM, w), lambda i: (jnp.minimum(i, n_tiles - 1), 0))
    prev = lambda w: pl.BlockSpec((TM, w), lambda i: (jnp.maximum(i - 1, 0), 0))
    in_specs = [cur(d), cur(Z_W), cur(RET_V_W),
                _const_spec(dintra.shape), _const_spec(cdec.shape),
                pl.BlockSpec(memory_space=pltpu.SMEM),
                _const_spec((d, d)), _const_spec((1, d)),
                _const_spec((1, d)), _const_spec((1, d)),
                _const_spec((d, D_FF)), _const_spec((d, D_FF)), _const_spec((D_FF, d)),
                prev(PLE_DIM), _const_spec((d, d)), _const_spec((PLE_DIM, d)), _const_spec((1, d))]
    return pl.pallas_call(
        functools.partial(_mix_ffn2_kernel, tiles_per_seq=seq // TM),
        grid=(n_tiles + 1,),
        in_specs=in_specs,
        out_specs=prev(d),
        out_shape=jax.ShapeDtypeStruct((t, d), F32),
        scratch_shapes=[
            pltpu.VMEM((TM, d), F32),
            pltpu.VMEM((TM, d), BF16),
            pltpu.VMEM((TM, d), F32),
            pltpu.VMEM((TM, d), BF16),
            pltpu.VMEM((RET_HEADS, CHUNK, RET_V_DIM), F32),
            pltpu.VMEM((CHUNK, 2 * SWA_KV_W), BF16),
            pltpu.VMEM((CHUNK, 2 * SWA_KV_W), BF16),
        ],
        compiler_params=pltpu.CompilerParams(
            dimension_semantics=("arbitrary",), vmem_limit_bytes=VMEM_LIMIT),
        name="mix_ffn2",
    )(h1, z, rg, dintra, cdec, sinks, w_out, mix_g, pre_g, post_g, wg, wu, wd, p, wpg, wpp, ple_g)


def _tables():
    half = RET_QK_DIM // 2
    inv_freq = ROPE_BASE ** (-jnp.arange(half, dtype=F32) / half)
    invf = jnp.tile(inv_freq, LANES // half)[None, :]
    sgn = jnp.tile(jnp.concatenate([-jnp.ones(half, F32), jnp.ones(half, F32)]), LANES // RET_QK_DIM)[None, :]
    log_gamma = jnp.log(1.0 - 2.0 ** (-5.0 - jnp.arange(RET_HEADS, dtype=F32)))
    idx = jnp.arange(CHUNK, dtype=F32)
    diff = idx[:, None] - idx[None, :]
    causal = diff >= 0
    dintra = jnp.where(causal[None], jnp.exp(log_gamma[:, None, None] * jnp.where(causal, diff, 0.0)[None]), 0.0)
    q_decay = jnp.exp(log_gamma[:, None] * (idx[None, :] + 1.0))
    k_decay = jnp.exp(log_gamma[:, None] * (CHUNK - 1.0 - idx[None, :]))
    chunk_decay = jnp.exp(log_gamma * CHUNK)

    def per_token(dec):
        x = jnp.repeat(jnp.transpose(dec)[:, :, None], RET_QK_DIM, axis=2).reshape(CHUNK, RET_QK_W)
        return jnp.tile(x, (TM // CHUNK, 1))

    cdec = jnp.broadcast_to(chunk_decay[:, None, None], (RET_HEADS, CHUNK, RET_V_DIM))
    return invf, sgn, dintra, per_token(q_decay), per_token(k_decay), cdec


def kernel(x, p, positions, ffn1_pre_g, ffn1_post_g, ffn1_w_gate, ffn1_w_up, ffn1_w_down, mix_pre_g, mix_post_g, w_in, b_in, swa_sinks, w_out, ffn2_pre_g, ffn2_post_g, ffn2_w_gate, ffn2_w_up, ffn2_w_down, ple_w_proj, ple_w_gate, ple_norm_g):
    batch, seq, d = x.shape
    depth = p.shape[0]
    assert d == D_MODEL and seq % TM == 0
    t = batch * seq
    h = x.reshape(t, d)
    pos = positions.reshape(t, 1)
    invf, sgn, dintra, qdec, kdec, cdec = _tables()
    bf = lambda w: w.astype(BF16)
    for i in range(depth):
        h1, z, rg = _ffn1_proj_call(
            h, pos, ffn1_pre_g[i][None], ffn1_post_g[i][None],
            bf(ffn1_w_gate[i]), bf(ffn1_w_up[i]), bf(ffn1_w_down[i]),
            mix_pre_g[i][None], bf(w_in[i]), b_in[i][None], invf, sgn, qdec, kdec)
        h = _mix_ffn2_call(
            h1, z, rg, dintra, cdec, swa_sinks[i], bf(w_out[i]), mix_post_g[i][None],
            ffn2_pre_g[i][None], ffn2_post_g[i][None],
            bf(ffn2_w_gate[i]), bf(ffn2_w_up[i]), bf(ffn2_w_down[i]),
            p[i].reshape(t, PLE_DIM), bf(ple_w_gate[i]), bf(ple_w_proj[i]), ple_norm_g[i][None], seq=seq)
    return h.reshape(batch, seq, d)
```

```python
import functools

import jax
import jax.numpy as jnp
from jax import lax
from jax.experimental import pallas as pl
from jax.experimental.pallas import tpu as pltpu

D_MODEL = 1024
PLE_DIM = 256
D_FF = 2816
RET_HEADS = 4
RET_QK_DIM = 64
RET_V_DIM = 128
CHUNK = 128
SWA_Q_HEADS = 8
SWA_KV_HEADS = 2
SWA_HEAD_DIM = 64
ROPE_BASE = 10000.0
EPS = 1e-6
NEG_INF = -1e30

RET_QK_W = RET_HEADS * RET_QK_DIM
RET_V_W = RET_HEADS * RET_V_DIM
SWA_Q_W = SWA_Q_HEADS * SWA_HEAD_DIM
SWA_KV_W = SWA_KV_HEADS * SWA_HEAD_DIM
IN_W = 2 * RET_QK_W + 2 * RET_V_W + SWA_Q_W + 2 * SWA_KV_W
OFF_RQ = 0
OFF_RK = OFF_RQ + RET_QK_W
OFF_RV = OFF_RK + RET_QK_W
OFF_RG = OFF_RV + RET_V_W
OFF_SQ = OFF_RG + RET_V_W
OFF_SK = OFF_SQ + SWA_Q_W
OFF_SV = OFF_SK + SWA_KV_W
Z_RQ = 0
Z_RQD = Z_RQ + RET_QK_W
Z_RK = Z_RQD + RET_QK_W
Z_RKD = Z_RK + RET_QK_W
Z_RV = Z_RKD + RET_QK_W
Z_SQ = Z_RV + RET_V_W
Z_SK = Z_SQ + SWA_Q_W
Z_SKX = Z_SK + SWA_KV_W
Z_SV = Z_SKX + SWA_KV_W
Z_SVX = Z_SV + SWA_KV_W
Z_W = Z_SVX + SWA_KV_W

LANES = 128
FF_CHUNK = 256
TM = 512
TRIG_ROWS = 64
FFN_SPLIT = 5
VMEM_LIMIT = 60 * 1024 * 1024

BF16 = jnp.bfloat16
F32 = jnp.float32


def _rms(x, g):
    return x * lax.rsqrt(jnp.mean(x * x, axis=-1, keepdims=True) + EPS) * g


def _const_spec(shape):
    nd = len(shape)
    return pl.BlockSpec(shape, lambda *_: (0,) * nd, pipeline_mode=pl.Buffered(1))


def _gate_up(a, wg_ref, wu_ref, c):
    cols = slice(c * FF_CHUNK, (c + 1) * FF_CHUNK)
    g = jnp.dot(a, wg_ref[:, cols], preferred_element_type=F32).astype(BF16)
    u = jnp.dot(a, wu_ref[:, cols], preferred_element_type=F32).astype(BF16)
    return g * jax.nn.sigmoid(g) * u


def _down(acc_ref, hm, wd_ref, c):
    d = jnp.dot(hm, wd_ref[c * FF_CHUNK:(c + 1) * FF_CHUNK, :], preferred_element_type=F32)
    if c == 0:
        acc_ref[...] = d
    else:
        acc_ref[...] += d


def _ffn_slices(acc_ref, a, wg_ref, wu_ref, wd_ref, lo, hi, between):
    hm_next = _gate_up(a, wg_ref, wu_ref, lo)
    for c in range(lo, hi):
        hm = hm_next
        if c in between:
            between[c]()
        if c + 1 < hi:
            hm_next = _gate_up(a, wg_ref, wu_ref, c + 1)
        _down(acc_ref, hm, wd_ref, c)


def _ffn1_proj_kernel(x_ref, pos_ref, pre_g_ref, post_g_ref, wg_ref, wu_ref, wd_ref,
                      mix_g_ref, w_in_ref, b_in_ref, invf_ref, sgn_ref, qdec_ref, kdec_ref,
                      h_ref, z_ref, rg_ref, acc_ref, u_ref, trig_ref):
    @pl.when(pl.program_id(0) == 0)
    def _():
        u_ref[...] = jnp.zeros_like(u_ref)

    u = u_ref[...]

    def store_trig(piece):
        rows = slice(piece * TRIG_ROWS, (piece + 1) * TRIG_ROWS)
        ang = pos_ref[rows, :].astype(F32) * invf_ref[...]
        trig_ref[0, rows, :] = jnp.cos(ang)
        trig_ref[1, rows, :] = jnp.sin(ang) * sgn_ref[...]

    def proj(off, width):
        cols = slice(off, off + width)
        return jnp.dot(u, w_in_ref[:, cols], preferred_element_type=F32) + b_in_ref[:, cols]

    def store_rotary():
        rqk = proj(OFF_RQ, 2 * RET_QK_W)
        cos, sin = trig_ref[0], trig_ref[1]
        lane = lax.broadcasted_iota(jnp.int32, (TM, LANES), 1)
        lo_half = (lane % RET_QK_DIM) < (RET_QK_DIM // 2)

        def rotary(v):
            swapped = jnp.where(lo_half, pltpu.roll(v, LANES - RET_QK_DIM // 2, 1),
                                pltpu.roll(v, RET_QK_DIM // 2, 1))
            return v * cos + swapped * sin

        for col in range(RET_QK_W // LANES):
            q = rotary(rqk[:, col * LANES:(col + 1) * LANES])
            k = rotary(rqk[:, RET_QK_W + col * LANES:RET_QK_W + (col + 1) * LANES]) * (RET_QK_DIM ** -0.5)
            cols = slice(col * LANES, (col + 1) * LANES)
            z_ref[:, Z_RQ + col * LANES:Z_RQ + (col + 1) * LANES] = q.astype(BF16)
            z_ref[:, Z_RK + col * LANES:Z_RK + (col + 1) * LANES] = k.astype(BF16)
            z_ref[:, Z_RQD + col * LANES:Z_RQD + (col + 1) * LANES] = (q * qdec_ref[:, cols]).astype(BF16)
            z_ref[:, Z_RKD + col * LANES:Z_RKD + (col + 1) * LANES] = (k * kdec_ref[:, cols]).astype(BF16)

    def store_swa_kv():
        skv = proj(OFF_SK, 2 * SWA_KV_W)
        z_ref[:, Z_SK:Z_SK + SWA_KV_W] = skv[:, :SWA_KV_W].astype(BF16)
        z_ref[:, Z_SV:Z_SV + SWA_KV_W] = skv[:, SWA_KV_W:].astype(BF16)
        z_ref[:, Z_SKX:Z_SKX + SWA_KV_W] = pltpu.roll(skv[:, :SWA_KV_W], SWA_HEAD_DIM, 1).astype(BF16)
        z_ref[:, Z_SVX:Z_SVX + SWA_KV_W] = pltpu.roll(skv[:, SWA_KV_W:], SWA_HEAD_DIM, 1).astype(BF16)

    def store_swa_q():
        z_ref[:, Z_SQ:Z_SQ + SWA_Q_W] = (proj(OFF_SQ, SWA_Q_W) * (SWA_HEAD_DIM ** -0.5)).astype(BF16)

    def store_ret_v():
        z_ref[:, Z_RV:Z_RV + RET_V_W] = proj(OFF_RV, RET_V_W).astype(BF16)

    n_trig = TM // TRIG_ROWS
    n_ffn = D_FF // FF_CHUNK
    assert n_trig + 1 < n_ffn
    extra = {1: store_swa_kv, 3: store_swa_q, n_trig: store_rotary}

    def after_slice(c):
        if c < n_trig:
            store_trig(c)
        if c in extra:
            extra[c]()

    rg_ref[...] = proj(OFF_RG, RET_V_W)
    x = x_ref[...]
    a = _rms(x, pre_g_ref[...]).astype(BF16)
    _ffn_slices(acc_ref, a, wg_ref, wu_ref, wd_ref, 0, n_ffn,
                {c: functools.partial(after_slice, c) for c in range(n_trig + 1)})
    store_ret_v()
    h = x + 0.5 * _rms(acc_ref[...], post_g_ref[...])
    h_ref[...] = h
    u_ref[...] = _rms(h, mix_g_ref[...]).astype(BF16)


def _ffn1_proj_call(x, pos, pre_g, post_g, wg, wu, wd, mix_g, w_in, b_in, invf, sgn, qdec, kdec):
    t, d = x.shape
    n_tiles = t // TM
    cur = lambda w: pl.BlockSpec((TM, w), lambda i: (jnp.minimum(i, n_tiles - 1), 0))
    prev = lambda w: pl.BlockSpec((TM, w), lambda i: (jnp.maximum(i - 1, 0), 0))
    in_specs = [cur(d), prev(1), _const_spec((1, d)), _const_spec((1, d)),
                _const_spec((d, D_FF)), _const_spec((d, D_FF)), _const_spec((D_FF, d)),
                _const_spec((1, d)), _const_spec((d, IN_W)), _const_spec((1, IN_W)),
                _const_spec((1, LANES)), _const_spec((1, LANES)),
                _const_spec((TM, RET_QK_W)), _const_spec((TM, RET_QK_W))]
    return pl.pallas_call(
        _ffn1_proj_kernel,
        grid=(n_tiles + 1,),
        in_specs=in_specs,
        out_specs=[cur(d), prev(Z_W), prev(RET_V_W)],
        out_shape=[jax.ShapeDtypeStruct((t, d), F32), jax.ShapeDtypeStruct((t, Z_W), BF16),
                   jax.ShapeDtypeStruct((t, RET_V_W), F32)],
        scratch_shapes=[pltpu.VMEM((TM, d), F32),
                        pltpu.VMEM((TM, d), BF16),
                        pltpu.VMEM((2, TM, LANES), F32)],
        compiler_params=pltpu.CompilerParams(
            dimension_semantics=("arbitrary",), vmem_limit_bytes=VMEM_LIMIT),
        name="ffn1_proj",
    )(x, pos, pre_g, post_g, wg, wu, wd, mix_g, w_in, b_in, invf, sgn, qdec, kdec)


class _Attention:
    def __init__(self, z_ref, rg_ref, dintra_ref, cdec_ref, sink_ref, mixin_ref, state_ref, kprev_ref, vprev_ref,
                 seq_tile):
        self.z_ref, self.rg_ref, self.dintra_ref, self.cdec_ref = z_ref, rg_ref, dintra_ref, cdec_ref
        self.sink_ref, self.mixin_ref, self.state_ref = sink_ref, mixin_ref, state_ref
        self.kprev_ref, self.vprev_ref, self.seq_tile = kprev_ref, vprev_ref, seq_tile
        lane = lax.broadcasted_iota(jnp.int32, (CHUNK, LANES), 1)
        self.lo_head = lane < RET_QK_DIM
        self.lo256 = lax.broadcasted_iota(jnp.int32, (2 * CHUNK, LANES), 1) < SWA_HEAD_DIM
        row = lax.broadcasted_iota(jnp.int32, (2 * CHUNK, 2 * CHUNK), 0)
        qi = row % CHUNK
        self.kj = lax.broadcasted_iota(jnp.int32, (2 * CHUNK, 2 * CHUNK), 1)
        self.band = (self.kj > qi) & (self.kj <= qi + CHUNK)
        self.top_rows = lax.broadcasted_iota(jnp.int32, (2 * CHUNK, 1), 0) < CHUNK

    def scores(self, c):
        z_ref = self.z_ref
        rows = slice(c * CHUNK, (c + 1) * CHUNK)
        zero = jnp.zeros((), BF16)
        nt = (((1,), (1,)), ((), ()))
        ret = []
        for col in range(RET_QK_W // LANES):
            q2 = z_ref[rows, Z_RQ + col * LANES:Z_RQ + (col + 1) * LANES]
            k2 = z_ref[rows, Z_RK + col * LANES:Z_RK + (col + 1) * LANES]
            k2d = z_ref[rows, Z_RKD + col * LANES:Z_RKD + (col + 1) * LANES]
            for sub in range(2):
                head = 2 * col + sub
                keep = self.lo_head if sub == 0 else ~self.lo_head
                kh = jnp.where(keep, k2, zero)
                khd = jnp.where(keep, k2d, zero)
                v = z_ref[rows, Z_RV + head * RET_V_DIM:Z_RV + (head + 1) * RET_V_DIM]
                s = lax.dot_general(q2, kh, nt, preferred_element_type=F32)
                kv = lax.dot_general(khd, v, (((0,), (0,)), ((), ())), preferred_element_type=F32)
                ret.append((s, kv, v))

        def window(off, prev_ref, lane_off):
            own = z_ref[rows, off:off + SWA_KV_W]
            if c == 0:
                prev = prev_ref[:, lane_off:lane_off + SWA_KV_W]
            else:
                prev = z_ref[(c - 1) * CHUNK:c * CHUNK, off:off + SWA_KV_W]
            return jnp.concatenate([prev, own], axis=0)

        kw, kwx = window(Z_SK, self.kprev_ref, 0), window(Z_SKX, self.kprev_ref, SWA_KV_W)
        vw, vwx = window(Z_SV, self.vprev_ref, 0), window(Z_SVX, self.vprev_ref, SWA_KV_W)
        swa = []
        for j in range(SWA_KV_HEADS):
            k_a, k_b = (kw, kwx) if j == 0 else (kwx, kw)
            v_a, v_b = (vw, vwx) if j == 0 else (vwx, vw)
            ks = (jnp.where(self.lo256, k_a, zero), jnp.where(self.lo256, zero, k_b))
            vs = (jnp.where(self.lo256, v_a, zero), jnp.where(self.lo256, zero, v_b))
            qcol = Z_SQ + 2 * j * LANES
            q4 = jnp.concatenate([z_ref[rows, qcol:qcol + LANES], z_ref[rows, qcol + LANES:qcol + 2 * LANES]],
                                 axis=0)
            sc = [lax.dot_general(q4, kk, nt, preferred_element_type=F32) for kk in ks]
            swa.append((sc, vs))
        return ret, swa

    def finish(self, c, staged):
        ret, swa = staged
        rows = slice(c * CHUNK, (c + 1) * CHUNK)
        for head, (s, kv, v) in enumerate(ret):
            col = head // 2
            q2d = self.z_ref[rows, Z_RQD + col * LANES:Z_RQD + (col + 1) * LANES]
            s = (s * self.dintra_ref[head]).astype(BF16)
            st = self.state_ref[head]
            y = (jnp.dot(s, v, preferred_element_type=F32)
                 + jnp.dot(q2d, st.astype(BF16), preferred_element_type=F32))
            self.state_ref[head] = self.cdec_ref[head] * st + kv
            mu = jnp.mean(y, axis=-1, keepdims=True)
            yc = y - mu
            var = jnp.mean(yc * yc, axis=-1, keepdims=True)
            yn = yc * lax.rsqrt(var + EPS)
            gte = self.rg_ref[rows, head * RET_V_DIM:(head + 1) * RET_V_DIM]
            self.mixin_ref[rows, head * RET_V_DIM:(head + 1) * RET_V_DIM] = (
                gte * jax.nn.sigmoid(gte) * yn).astype(BF16)

        first = jnp.logical_and(self.seq_tile == 0, c == 0)
        key_lo = jnp.where(first, CHUNK, 0)
        valid = self.band & (self.kj >= key_lo)
        for j, (scs, vs) in enumerate(swa):
            outs, invs = [], []
            for sub, (sc, vv) in enumerate(zip(scs, vs)):
                sc = jnp.where(valid, sc, NEG_INF)
                sink = jnp.where(self.top_rows, self.sink_ref[4 * j + sub], self.sink_ref[4 * j + 2 + sub])
                m = jnp.maximum(jnp.max(sc, axis=-1, keepdims=True), sink)
                e = jnp.exp(sc - m)
                den = jnp.sum(e, axis=-1, keepdims=True) + jnp.exp(sink - m)
                outs.append(jnp.dot(e.astype(BF16), vv, preferred_element_type=F32))
                invs.append(1.0 / den)
            o4 = (outs[0] + outs[1]) * jnp.where(self.lo256, invs[0], invs[1])
            for col in range(2):
                ocol = RET_V_W + (2 * j + col) * LANES
                self.mixin_ref[rows, ocol:ocol + LANES] = o4[col * CHUNK:(col + 1) * CHUNK].astype(BF16)

    def carry(self):
        last = slice(TM - CHUNK, TM)
        z_ref = self.z_ref
        self.kprev_ref[:, :SWA_KV_W] = z_ref[last, Z_SK:Z_SK + SWA_KV_W]
        self.kprev_ref[:, SWA_KV_W:] = z_ref[last, Z_SKX:Z_SKX + SWA_KV_W]
        self.vprev_ref[:, :SWA_KV_W] = z_ref[last, Z_SV:Z_SV + SWA_KV_W]
        self.vprev_ref[:, SWA_KV_W:] = z_ref[last, Z_SVX:Z_SVX + SWA_KV_W]


def _mix_ffn2_kernel(h1_ref, z_ref, rg_ref, dintra_ref, cdec_ref, sink_ref, w_out_ref, mix_g_ref,
                     pre_g_ref, post_g_ref, wg_ref, wu_ref, wd_ref, p_ref, wpg_ref, wpp_ref, ple_g_ref,
                     o_ref, h2_ref, a_ref, acc_ref, mixin_ref, state_ref, kprev_ref, vprev_ref, *, tiles_per_seq):
    step = pl.program_id(0)
    n_tiles = pl.num_programs(0) - 1
    seq_tile = jnp.minimum(step, n_tiles - 1) % tiles_per_seq

    @pl.when(step == 0)
    def _():
        h2_ref[...] = jnp.zeros_like(h2_ref)
        a_ref[...] = jnp.zeros_like(a_ref)
        acc_ref[...] = jnp.zeros_like(acc_ref)

    @pl.when(seq_tile == 0)
    def _():
        state_ref[...] = jnp.zeros_like(state_ref)
        kprev_ref[...] = jnp.zeros_like(kprev_ref)
        vprev_ref[...] = jnp.zeros_like(vprev_ref)

    att = _Attention(z_ref, rg_ref, dintra_ref, cdec_ref, sink_ref, mixin_ref, state_ref, kprev_ref, vprev_ref,
                     seq_tile)
    n_att = TM // CHUNK
    n_ffn = D_FF // FF_CHUNK
    assert n_att < n_ffn - FFN_SPLIT
    staged = [att.scores(0)]
    side = {}

    def attention_chunk(k):
        att.finish(k, staged[0])
        if k + 1 < n_att:
            staged[0] = att.scores(k + 1)

    def after_attention():
        att.carry()
        side["proj"] = jnp.dot(p_ref[...].astype(BF16), wpp_ref[...], preferred_element_type=F32)

    between = {FFN_SPLIT + k: functools.partial(attention_chunk, k) for k in range(n_att)}
    between[FFN_SPLIT + n_att] = after_attention
    _ffn_slices(acc_ref, a_ref[...], wg_ref, wu_ref, wd_ref, FFN_SPLIT, n_ffn, between)

    mix = jnp.dot(mixin_ref[...], w_out_ref[...], preferred_element_type=F32)
    h = h2_ref[...] + 0.5 * _rms(acc_ref[...], post_g_ref[...])
    gate = jax.nn.sigmoid(jnp.dot(h.astype(BF16), wpg_ref[...], preferred_element_type=F32))
    h2 = h1_ref[...] + _rms(mix, mix_g_ref[...])
    h2_ref[...] = h2
    a = _rms(h2, pre_g_ref[...]).astype(BF16)
    a_ref[...] = a

    def store_out():
        o_ref[...] = h + _rms(gate * side["proj"], ple_g_ref[...])

    _ffn_slices(acc_ref, a, wg_ref, wu_ref, wd_ref, 0, FFN_SPLIT, {0: store_out})


def _mix_ffn2_call(h1, z, rg, dintra, cdec, sinks, w_out, mix_g, pre_g, post_g, wg, wu, wd,
                   p, wpg, wpp, ple_g, *, seq):
    t, d = h1.shape
    n_tiles = t // TM
    cur = lambda w: pl.BlockSpec((TM, w), lambda i: (jnp.minimum(i, n_tiles - 1), 0))
    prev = lambda w: pl.BlockSpec((TM, w), lambda i: (jnp.maximum(i - 1, 0), 0))
    in_specs = [cur(d), cur(Z_W), cur(RET_V_W),
                _const_spec(dintra.shape), _const_spec(cdec.shape),
                pl.BlockSpec(memory_space=pltpu.SMEM),
                _const_spec((d, d)), _const_spec((1, d)),
                _const_spec((1, d)), _const_spec((1, d)),
                _const_spec((d, D_FF)), _const_spec((d, D_FF)), _const_spec((D_FF, d)),
                prev(PLE_DIM), _const_spec((d, d)), _const_spec((PLE_DIM, d)), _const_spec((1, d))]
    return pl.pallas_call(
        functools.partial(_mix_ffn2_kernel, tiles_per_seq=seq // TM),
        grid=(n_tiles + 1,),
        in_specs=in_specs,
        out_specs=prev(d),
        out_shape=jax.ShapeDtypeStruct((t, d), F32),
        scratch_shapes=[
            pltpu.VMEM((TM, d), F32),
            pltpu.VMEM((TM, d), BF16),
            pltpu.VMEM((TM, d), F32),
            pltpu.VMEM((TM, d), BF16),
            pltpu.VMEM((RET_HEADS, CHUNK, RET_V_DIM), F32),
            pltpu.VMEM((CHUNK, 2 * SWA_KV_W), BF16),
            pltpu.VMEM((CHUNK, 2 * SWA_KV_W), BF16),
        ],
        compiler_params=pltpu.CompilerParams(
            dimension_semantics=("arbitrary",), vmem_limit_bytes=VMEM_LIMIT),
        name="mix_ffn2",
    )(h1, z, rg, dintra, cdec, sinks, w_out, mix_g, pre_g, post_g, wg, wu, wd, p, wpg, wpp, ple_g)


def _tables():
    half = RET_QK_DIM // 2
    inv_freq = ROPE_BASE ** (-jnp.arange(half, dtype=F32) / half)
    invf = jnp.tile(inv_freq, LANES // half)[None, :]
    sgn = jnp.tile(jnp.concatenate([-jnp.ones(half, F32), jnp.ones(half, F32)]), LANES // RET_QK_DIM)[None, :]
    log_gamma = jnp.log(1.0 - 2.0 ** (-5.0 - jnp.arange(RET_HEADS, dtype=F32)))
    idx = jnp.arange(CHUNK, dtype=F32)
    diff = idx[:, None] - idx[None, :]
    causal = diff >= 0
    dintra = jnp.where(causal[None], jnp.exp(log_gamma[:, None, None] * jnp.where(causal, diff, 0.0)[None]), 0.0)
    q_decay = jnp.exp(log_gamma[:, None] * (idx[None, :] + 1.0))
    k_decay = jnp.exp(log_gamma[:, None] * (CHUNK - 1.0 - idx[None, :]))
    chunk_decay = jnp.exp(log_gamma * CHUNK)

    def per_token(dec):
        x = jnp.repeat(jnp.transpose(dec)[:, :, None], RET_QK_DIM, axis=2).reshape(CHUNK, RET_QK_W)
        return jnp.tile(x, (TM // CHUNK, 1))

    cdec = jnp.broadcast_to(chunk_decay[:, None, None], (RET_HEADS, CHUNK, RET_V_DIM))
    return invf, sgn, dintra, per_token(q_decay), per_token(k_decay), cdec


def kernel(x, p, positions, ffn1_pre_g, ffn1_post_g, ffn1_w_gate, ffn1_w_up, ffn1_w_down, mix_pre_g, mix_post_g, w_in, b_in, swa_sinks, w_out, ffn2_pre_g, ffn2_post_g, ffn2_w_gate, ffn2_w_up, ffn2_w_down, ple_w_proj, ple_w_gate, ple_norm_g):
    batch, seq, d = x.shape
    depth = p.shape[0]
    assert d == D_MODEL and seq % TM == 0
    t = batch * seq
    h = x.reshape(t, d)
    pos = positions.reshape(t, 1)
    invf, sgn, dintra, qdec, kdec, cdec = _tables()
    bf = lambda w: w.astype(BF16)
    for i in range(depth):
        h1, z, rg = _ffn1_proj_call(
            h, pos, ffn1_pre_g[i][None], ffn1_post_g[i][None],
            bf(ffn1_w_gate[i]), bf(ffn1_w_up[i]), bf(ffn1_w_down[i]),
            mix_pre_g[i][None], bf(w_in[i]), b_in[i][None], invf, sgn, qdec, kdec)
        h = _mix_ffn2_call(
            h1, z, rg, dintra, cdec, swa_sinks[i], bf(w_out[i]), mix_post_g[i][None],
            ffn2_pre_g[i][None], ffn2_post_g[i][None],
            bf(ffn2_w_gate[i]), bf(ffn2_w_up[i]), bf(ffn2_w_down[i]),
            p[i].reshape(t, PLE_DIM), bf(ple_w_gate[i]), bf(ple_w_proj[i]), ple_norm_g[i][None], seq=seq)
    return h.reshape(batch, seq, d)
```

```python
import functools

import jax
import jax.numpy as jnp
import numpy as np
from jax import lax
from jax.experimental import pallas as pl
from jax.experimental.pallas import tpu as pltpu

D_MODEL = 1024
PLE_DIM = 256
D_FF = 2816
RET_HEADS = 4
RET_QK_DIM = 64
RET_V_DIM = 128
CHUNK = 128
SWA_Q_HEADS = 8
SWA_KV_HEADS = 2
SWA_HEAD_DIM = 64
ROPE_BASE = 10000.0
EPS = 1e-6
NEG_INF = -1e30

RET_QK_W = RET_HEADS * RET_QK_DIM
RET_V_W = RET_HEADS * RET_V_DIM
SWA_Q_W = SWA_Q_HEADS * SWA_HEAD_DIM
SWA_KV_W = SWA_KV_HEADS * SWA_HEAD_DIM
IN_W = 2 * RET_QK_W + 2 * RET_V_W + SWA_Q_W + 2 * SWA_KV_W
OFF_RQ = 0
OFF_RK = OFF_RQ + RET_QK_W
OFF_RV = OFF_RK + RET_QK_W
OFF_RG = OFF_RV + RET_V_W
OFF_SQ = OFF_RG + RET_V_W
OFF_SK = OFF_SQ + SWA_Q_W
OFF_SV = OFF_SK + SWA_KV_W
Z_RQ = 0
Z_RQD = Z_RQ + RET_QK_W
Z_RK = Z_RQD + RET_QK_W
Z_RKD = Z_RK + RET_QK_W
Z_RV = Z_RKD + RET_QK_W
Z_SQ = Z_RV + RET_V_W
Z_SK = Z_SQ + SWA_Q_W
Z_SKX = Z_SK + SWA_KV_W
Z_SV = Z_SKX + SWA_KV_W
Z_SVX = Z_SV + SWA_KV_W
Z_W = Z_SVX + SWA_KV_W

LANES = 128
BF16_ROWS = 16
FF_EDGES = tuple(range(0, D_FF + 1, 256))
TM = 512
TRIG_ROWS = 64
FFN_SPLIT = 5
VMEM_LIMIT = 60 * 1024 * 1024

BF16 = jnp.bfloat16
F32 = jnp.float32


def _rms(x, g):
    return x * lax.rsqrt(jnp.mean(x * x, axis=-1, keepdims=True) + EPS) * g


def _const_spec(shape):
    nd = len(shape)
    return pl.BlockSpec(shape, lambda *_: (0,) * nd, pipeline_mode=pl.Buffered(1))


def _gate_up(a, wg_ref, wu_ref, c):
    cols = slice(FF_EDGES[c], FF_EDGES[c + 1])
    g = jnp.dot(a, wg_ref[:, cols], preferred_element_type=F32).astype(BF16)
    u = jnp.dot(a, wu_ref[:, cols], preferred_element_type=F32).astype(BF16)
    return g * jax.nn.sigmoid(g) * u


def _down(acc_ref, hm, wd_ref, c):
    d = jnp.dot(hm, wd_ref[FF_EDGES[c]:FF_EDGES[c + 1], :], preferred_element_type=F32)
    if c == 0:
        acc_ref[...] = d
    else:
        acc_ref[...] += d


def _ffn_slices(acc_ref, a, wg_ref, wu_ref, wd_ref, lo, hi, between):
    hm_next = _gate_up(a, wg_ref, wu_ref, lo)
    for c in range(lo, hi):
        hm = hm_next
        if c in between:
            between[c]()
        if c + 1 < hi:
            hm_next = _gate_up(a, wg_ref, wu_ref, c + 1)
        _down(acc_ref, hm, wd_ref, c)


def _ffn1_proj_kernel(x_ref, pos_ref, pre_g_ref, post_g_ref, wg_ref, wu_ref, wd_ref,
                      mix_g_ref, w_in_ref, b_in_ref, invf_ref, sgn_ref, qdec_ref, kdec_ref, *rest, n_cast):
    cast_in, (h_ref, z_ref, rg_ref) = rest[:n_cast], rest[n_cast:n_cast + 3]
    cast_out, (acc_ref, u_ref, trig_ref) = rest[n_cast + 3:2 * n_cast + 3], rest[2 * n_cast + 3:]

    @pl.when(pl.program_id(0) == 0)
    def _():
        u_ref[...] = jnp.zeros_like(u_ref)

    for src_ref, dst_ref in zip(cast_in, cast_out):
        dst_ref[...] = src_ref[...].astype(BF16)

    u = u_ref[...]

    def store_trig(piece):
        rows = slice(piece * TRIG_ROWS, (piece + 1) * TRIG_ROWS)
        ang = pos_ref[rows, :].astype(F32) * invf_ref[...]
        trig_ref[0, rows, :] = jnp.cos(ang)
        trig_ref[1, rows, :] = jnp.sin(ang) * sgn_ref[...]

    def proj(off, width):
        cols = slice(off, off + width)
        return jnp.dot(u, w_in_ref[:, cols], preferred_element_type=F32) + b_in_ref[:, cols]

    def store_rotary():
        rqk = proj(OFF_RQ, 2 * RET_QK_W)
        cos, sin = trig_ref[0], trig_ref[1]
        lane = lax.broadcasted_iota(jnp.int32, (TM, LANES), 1)
        lo_half = (lane % RET_QK_DIM) < (RET_QK_DIM // 2)

        def rotary(v):
            swapped = jnp.where(lo_half, pltpu.roll(v, LANES - RET_QK_DIM // 2, 1),
                                pltpu.roll(v, RET_QK_DIM // 2, 1))
            return v * cos + swapped * sin

        for col in range(RET_QK_W // LANES):
            q = rotary(rqk[:, col * LANES:(col + 1) * LANES])
            k = rotary(rqk[:, RET_QK_W + col * LANES:RET_QK_W + (col + 1) * LANES]) * (RET_QK_DIM ** -0.5)
            cols = slice(col * LANES, (col + 1) * LANES)
            z_ref[:, Z_RQ + col * LANES:Z_RQ + (col + 1) * LANES] = q.astype(BF16)
            z_ref[:, Z_RK + col * LANES:Z_RK + (col + 1) * LANES] = k.astype(BF16)
            z_ref[:, Z_RQD + col * LANES:Z_RQD + (col + 1) * LANES] = (q * qdec_ref[:, cols]).astype(BF16)
            z_ref[:, Z_RKD + col * LANES:Z_RKD + (col + 1) * LANES] = (k * kdec_ref[:, cols]).astype(BF16)

    def store_swa_kv():
        skv = proj(OFF_SK, 2 * SWA_KV_W)
        z_ref[:, Z_SK:Z_SK + SWA_KV_W] = skv[:, :SWA_KV_W].astype(BF16)
        z_ref[:, Z_SV:Z_SV + SWA_KV_W] = skv[:, SWA_KV_W:].astype(BF16)
        z_ref[:, Z_SKX:Z_SKX + SWA_KV_W] = pltpu.roll(skv[:, :SWA_KV_W], SWA_HEAD_DIM, 1).astype(BF16)
        z_ref[:, Z_SVX:Z_SVX + SWA_KV_W] = pltpu.roll(skv[:, SWA_KV_W:], SWA_HEAD_DIM, 1).astype(BF16)

    def store_swa_q():
        z_ref[:, Z_SQ:Z_SQ + SWA_Q_W] = (proj(OFF_SQ, SWA_Q_W) * (SWA_HEAD_DIM ** -0.5)).astype(BF16)

    def store_ret_v():
        z_ref[:, Z_RV:Z_RV + RET_V_W] = proj(OFF_RV, RET_V_W).astype(BF16)

    n_trig = TM // TRIG_ROWS
    n_ffn = len(FF_EDGES) - 1
    assert n_trig + 1 < n_ffn
    extra = {1: store_swa_kv, 3: store_swa_q, n_trig: store_rotary}

    def after_slice(c):
        if c < n_trig:
            store_trig(c)
        if c in extra:
            extra[c]()

    rg_ref[...] = proj(OFF_RG, RET_V_W)
    x = x_ref[...]
    a = _rms(x, pre_g_ref[...]).astype(BF16)
    _ffn_slices(acc_ref, a, wg_ref, wu_ref, wd_ref, 0, n_ffn,
                {c: functools.partial(after_slice, c) for c in range(n_trig + 1)})
    store_ret_v()
    h = x + 0.5 * _rms(acc_ref[...], post_g_ref[...])
    h_ref[...] = h
    u_ref[...] = _rms(h, mix_g_ref[...]).astype(BF16)


def _cast_spec(shape, n_steps):
    rows, cols = shape
    block = next(b for b in range(BF16_ROWS, rows + 1, BF16_ROWS) if rows % b == 0 and rows // b <= n_steps)
    return pl.BlockSpec((block, cols), lambda i: (jnp.minimum(i, rows // block - 1), 0))


def _ffn1_proj_call(x, pos, pre_g, post_g, wg, wu, wd, mix_g, w_in, b_in, invf, sgn, qdec, kdec, to_cast):
    t, d = x.shape
    n_tiles = t // TM
    cur = lambda w: pl.BlockSpec((TM, w), lambda i: (jnp.minimum(i, n_tiles - 1), 0))
    prev = lambda w: pl.BlockSpec((TM, w), lambda i: (jnp.maximum(i - 1, 0), 0))
    cast_specs = [_cast_spec(w.shape, n_tiles) for w in to_cast]
    in_specs = [cur(d), prev(1), _const_spec((1, d)), _const_spec((1, d)),
                _const_spec((d, D_FF)), _const_spec((d, D_FF)), _const_spec((D_FF, d)),
                _const_spec((1, d)), _const_spec((d, IN_W)), _const_spec((1, IN_W)),
                _const_spec((1, LANES)), _const_spec((1, LANES)),
                _const_spec((TM, RET_QK_W)), _const_spec((TM, RET_QK_W))] + cast_specs
    h1, z, rg, *cast = pl.pallas_call(
        functools.partial(_ffn1_proj_kernel, n_cast=len(to_cast)),
        grid=(n_tiles + 1,),
        in_specs=in_specs,
        out_specs=[cur(d), prev(Z_W), prev(RET_V_W)] + cast_specs,
        out_shape=[jax.ShapeDtypeStruct((t, d), F32), jax.ShapeDtypeStruct((t, Z_W), BF16),
                   jax.ShapeDtypeStruct((t, RET_V_W), F32)]
                  + [jax.ShapeDtypeStruct(w.shape, BF16) for w in to_cast],
        scratch_shapes=[pltpu.VMEM((TM, d), F32),
                        pltpu.VMEM((TM, d), BF16),
                        pltpu.VMEM((2, TM, LANES), F32)],
        compiler_params=pltpu.CompilerParams(
            dimension_semantics=("arbitrary",), vmem_limit_bytes=VMEM_LIMIT),
        name="ffn1_proj",
    )(x, pos, pre_g, post_g, wg, wu, wd, mix_g, w_in, b_in, invf, sgn, qdec, kdec, *to_cast)
    return h1, z, rg, cast


class _Attention:
    def __init__(self, z_ref, rg_ref, dintra_ref, cdec_ref, sink_ref, mixin_ref, state_ref, kprev_ref, vprev_ref,
                 seq_tile):
        self.z_ref, self.rg_ref, self.dintra_ref, self.cdec_ref = z_ref, rg_ref, dintra_ref, cdec_ref
        self.sink_ref, self.mixin_ref, self.state_ref = sink_ref, mixin_ref, state_ref
        self.kprev_ref, self.vprev_ref, self.seq_tile = kprev_ref, vprev_ref, seq_tile
        lane = lax.broadcasted_iota(jnp.int32, (CHUNK, LANES), 1)
        self.lo_head = lane < RET_QK_DIM
        self.lo256 = lax.broadcasted_iota(jnp.int32, (2 * CHUNK, LANES), 1) < SWA_HEAD_DIM
        row = lax.broadcasted_iota(jnp.int32, (2 * CHUNK, 2 * CHUNK), 0)
        qi = row % CHUNK
        self.kj = lax.broadcasted_iota(jnp.int32, (2 * CHUNK, 2 * CHUNK), 1)
        self.band = (self.kj > qi) & (self.kj <= qi + CHUNK)
        self.top_rows = lax.broadcasted_iota(jnp.int32, (2 * CHUNK, 1), 0) < CHUNK

    def scores(self, c):
        z_ref = self.z_ref
        rows = slice(c * CHUNK, (c + 1) * CHUNK)
        zero = jnp.zeros((), BF16)
        nt = (((1,), (1,)), ((), ()))
        ret = []
        for col in range(RET_QK_W // LANES):
            q2 = z_ref[rows, Z_RQ + col * LANES:Z_RQ + (col + 1) * LANES]
            k2 = z_ref[rows, Z_RK + col * LANES:Z_RK + (col + 1) * LANES]
            k2d = z_ref[rows, Z_RKD + col * LANES:Z_RKD + (col + 1) * LANES]
            for sub in range(2):
                head = 2 * col + sub
                keep = self.lo_head if sub == 0 else ~self.lo_head
                kh = jnp.where(keep, k2, zero)
                khd = jnp.where(keep, k2d, zero)
                v = z_ref[rows, Z_RV + head * RET_V_DIM:Z_RV + (head + 1) * RET_V_DIM]
                s = lax.dot_general(q2, kh, nt, preferred_element_type=F32)
                kv = lax.dot_general(khd, v, (((0,), (0,)), ((), ())), preferred_element_type=F32)
                ret.append((s, kv, v))

        def window(off, prev_ref, lane_off):
            own = z_ref[rows, off:off + SWA_KV_W]
            if c == 0:
                prev = prev_ref[:, lane_off:lane_off + SWA_KV_W]
            else:
                prev = z_ref[(c - 1) * CHUNK:c * CHUNK, off:off + SWA_KV_W]
            return jnp.concatenate([prev, own], axis=0)

        kw, kwx = window(Z_SK, self.kprev_ref, 0), window(Z_SKX, self.kprev_ref, SWA_KV_W)
        vw, vwx = window(Z_SV, self.vprev_ref, 0), window(Z_SVX, self.vprev_ref, SWA_KV_W)
        swa = []
        for j in range(SWA_KV_HEADS):
            k_a, k_b = (kw, kwx) if j == 0 else (kwx, kw)
            v_a, v_b = (vw, vwx) if j == 0 else (vwx, vw)
            ks = (jnp.where(self.lo256, k_a, zero), jnp.where(self.lo256, zero, k_b))
            vs = (jnp.where(self.lo256, v_a, zero), jnp.where(self.lo256, zero, v_b))
            qcol = Z_SQ + 2 * j * LANES
            q4 = jnp.concatenate([z_ref[rows, qcol:qcol + LANES], z_ref[rows, qcol + LANES:qcol + 2 * LANES]],
                                 axis=0)
            sc = [lax.dot_general(q4, kk, nt, preferred_element_type=F32) for kk in ks]
            swa.append((sc, vs))
        return ret, swa

    def finish(self, c, staged):
        ret, swa = staged
        rows = slice(c * CHUNK, (c + 1) * CHUNK)
        for head, (s, kv, v) in enumerate(ret):
            col = head // 2
            q2d = self.z_ref[rows, Z_RQD + col * LANES:Z_RQD + (col + 1) * LANES]
            s = (s * self.dintra_ref[head]).astype(BF16)
            st = self.state_ref[head]
            y = (jnp.dot(s, v, preferred_element_type=F32)
                 + jnp.dot(q2d, st.astype(BF16), preferred_element_type=F32))
            self.state_ref[head] = self.cdec_ref[head] * st + kv
            mu = jnp.mean(y, axis=-1, keepdims=True)
            yc = y - mu
            var = jnp.mean(yc * yc, axis=-1, keepdims=True)
            yn = yc * lax.rsqrt(var + EPS)
            gte = self.rg_ref[rows, head * RET_V_DIM:(head + 1) * RET_V_DIM]
            self.mixin_ref[rows, head * RET_V_DIM:(head + 1) * RET_V_DIM] = (
                gte * jax.nn.sigmoid(gte) * yn).astype(BF16)

        first = jnp.logical_and(self.seq_tile == 0, c == 0)
        key_lo = jnp.where(first, CHUNK, 0)
        valid = self.band & (self.kj >= key_lo)
        for j, (scs, vs) in enumerate(swa):
            outs, invs = [], []
            for sub, (sc, vv) in enumerate(zip(scs, vs)):
                sc = jnp.where(valid, sc, NEG_INF)
                sink = jnp.where(self.top_rows, self.sink_ref[4 * j + sub], self.sink_ref[4 * j + 2 + sub])
                m = jnp.maximum(jnp.max(sc, axis=-1, keepdims=True), sink)
                e = jnp.exp(sc - m)
                den = jnp.sum(e, axis=-1, keepdims=True) + jnp.exp(sink - m)
                outs.append(jnp.dot(e.astype(BF16), vv, preferred_element_type=F32))
                invs.append(1.0 / den)
            o4 = (outs[0] + outs[1]) * jnp.where(self.lo256, invs[0], invs[1])
            for col in range(2):
                ocol = RET_V_W + (2 * j + col) * LANES
                self.mixin_ref[rows, ocol:ocol + LANES] = o4[col * CHUNK:(col + 1) * CHUNK].astype(BF16)

    def carry(self):
        last = slice(TM - CHUNK, TM)
        z_ref = self.z_ref
        self.kprev_ref[:, :SWA_KV_W] = z_ref[last, Z_SK:Z_SK + SWA_KV_W]
        self.kprev_ref[:, SWA_KV_W:] = z_ref[last, Z_SKX:Z_SKX + SWA_KV_W]
        self.vprev_ref[:, :SWA_KV_W] = z_ref[last, Z_SV:Z_SV + SWA_KV_W]
        self.vprev_ref[:, SWA_KV_W:] = z_ref[last, Z_SVX:Z_SVX + SWA_KV_W]


def _mix_ffn2_kernel(h1_ref, z_ref, rg_ref, dintra_ref, cdec_ref, sink_ref, w_out_ref, mix_g_ref,
                     pre_g_ref, post_g_ref, wg_ref, wu_ref, wd_ref, p_ref, wpg_ref, wpp_ref, ple_g_ref,
                     o_ref, h2_ref, a_ref, acc_ref, mixin_ref, state_ref, kprev_ref, vprev_ref, *, tiles_per_seq):
    step = pl.program_id(0)
    n_tiles = pl.num_programs(0) - 1
    seq_tile = jnp.minimum(step, n_tiles - 1) % tiles_per_seq

    @pl.when(step == 0)
    def _():
        h2_ref[...] = jnp.zeros_like(h2_ref)
        a_ref[...] = jnp.zeros_like(a_ref)
        acc_ref[...] = jnp.zeros_like(acc_ref)

    @pl.when(seq_tile == 0)
    def _():
        state_ref[...] = jnp.zeros_like(state_ref)
        kprev_ref[...] = jnp.zeros_like(kprev_ref)
        vprev_ref[...] = jnp.zeros_like(vprev_ref)

    att = _Attention(z_ref, rg_ref, dintra_ref, cdec_ref, sink_ref, mixin_ref, state_ref, kprev_ref, vprev_ref,
                     seq_tile)
    n_att = TM // CHUNK
    n_ffn = len(FF_EDGES) - 1
    assert n_att < n_ffn - FFN_SPLIT
    staged = [att.scores(0)]
    side = {}

    def attention_chunk(k):
        att.finish(k, staged[0])
        if k + 1 < n_att:
            staged[0] = att.scores(k + 1)

    def after_attention():
        att.carry()
        side["proj"] = jnp.dot(p_ref[...].astype(BF16), wpp_ref[...], preferred_element_type=F32)

    between = {FFN_SPLIT + k: functools.partial(attention_chunk, k) for k in range(n_att)}
    between[FFN_SPLIT + n_att] = after_attention
    _ffn_slices(acc_ref, a_ref[...], wg_ref, wu_ref, wd_ref, FFN_SPLIT, n_ffn, between)

    mix = jnp.dot(mixin_ref[...], w_out_ref[...], preferred_element_type=F32)
    h = h2_ref[...] + 0.5 * _rms(acc_ref[...], post_g_ref[...])
    gate = jax.nn.sigmoid(jnp.dot(h.astype(BF16), wpg_ref[...], preferred_element_type=F32))
    h2 = h1_ref[...] + _rms(mix, mix_g_ref[...])
    h2_ref[...] = h2
    a = _rms(h2, pre_g_ref[...]).astype(BF16)
    a_ref[...] = a

    def store_out():
        o_ref[...] = h + _rms(gate * side["proj"], ple_g_ref[...])

    _ffn_slices(acc_ref, a, wg_ref, wu_ref, wd_ref, 0, FFN_SPLIT, {0: store_out})


def _mix_ffn2_call(h1, z, rg, dintra, cdec, sinks, w_out, mix_g, pre_g, post_g, wg, wu, wd,
                   p, wpg, wpp, ple_g, *, seq):
    t, d = h1.shape
    n_tiles = t // TM
    cur = lambda w: pl.BlockSpec((TM, w), lambda i: (jnp.minimum(i, n_tiles - 1), 0))
    prev = lambda w: pl.BlockSpec((TM, w), lambda i: (jnp.maximum(i - 1, 0), 0))
    in_specs = [cur(d), cur(Z_W), cur(RET_V_W),
                _const_spec(dintra.shape), _const_spec(cdec.shape),
                pl.BlockSpec(memory_space=pltpu.SMEM),
                _const_spec((d, d)), _const_spec((1, d)),
                _const_spec((1, d)), _const_spec((1, d)),
                _const_spec((d, D_FF)), _const_spec((d, D_FF)), _const_spec((D_FF, d)),
                prev(PLE_DIM), _const_spec((d, d)), _const_spec((PLE_DIM, d)), _const_spec((1, d))]
    return pl.pallas_call(
        functools.partial(_mix_ffn2_kernel, tiles_per_seq=seq // TM),
        grid=(n_tiles + 1,),
        in_specs=in_specs,
        out_specs=prev(d),
        out_shape=jax.ShapeDtypeStruct((t, d), F32),
        scratch_shapes=[
            pltpu.VMEM((TM, d), F32),
            pltpu.VMEM((TM, d), BF16),
            pltpu.VMEM((TM, d), F32),
            pltpu.VMEM((TM, d), BF16),
            pltpu.VMEM((RET_HEADS, CHUNK, RET_V_DIM), F32),
            pltpu.VMEM((CHUNK, 2 * SWA_KV_W), BF16),
            pltpu.VMEM((CHUNK, 2 * SWA_KV_W), BF16),
        ],
        compiler_params=pltpu.CompilerParams(
            dimension_semantics=("arbitrary",), vmem_limit_bytes=VMEM_LIMIT),
        name="mix_ffn2",
    )(h1, z, rg, dintra, cdec, sinks, w_out, mix_g, pre_g, post_g, wg, wu, wd, p, wpg, wpp, ple_g)


def _tables():
    f32 = np.float32
    half = RET_QK_DIM // 2
    inv_freq = f32(ROPE_BASE) ** (-np.arange(half, dtype=f32) / f32(half))
    invf = np.tile(inv_freq, LANES // half)[None, :]
    sgn = np.tile(np.concatenate([-np.ones(half, f32), np.ones(half, f32)]), LANES // RET_QK_DIM)[None, :]
    log_gamma = np.log(f32(1.0) - f32(2.0) ** (f32(-5.0) - np.arange(RET_HEADS, dtype=f32)))
    idx = np.arange(CHUNK, dtype=f32)
    diff = idx[:, None] - idx[None, :]
    causal = diff >= 0
    dintra = np.where(causal[None], np.exp(log_gamma[:, None, None] * np.where(causal, diff, f32(0.0))[None]),
                      f32(0.0)).astype(f32)
    q_decay = np.exp(log_gamma[:, None] * (idx[None, :] + f32(1.0)))
    k_decay = np.exp(log_gamma[:, None] * (f32(CHUNK - 1.0) - idx[None, :]))
    chunk_decay = np.exp(log_gamma * f32(CHUNK))

    def per_token(dec):
        x = np.repeat(np.transpose(dec)[:, :, None], RET_QK_DIM, axis=2).reshape(CHUNK, RET_QK_W)
        return np.tile(x, (TM // CHUNK, 1)).astype(f32)

    cdec = np.broadcast_to(chunk_decay[:, None, None], (RET_HEADS, CHUNK, RET_V_DIM)).astype(f32)
    return invf.astype(f32), sgn, dintra, per_token(q_decay), per_token(k_decay), cdec


def kernel(x, p, positions, ffn1_pre_g, ffn1_post_g, ffn1_w_gate, ffn1_w_up, ffn1_w_down, mix_pre_g, mix_post_g, w_in, b_in, swa_sinks, w_out, ffn2_pre_g, ffn2_post_g, ffn2_w_gate, ffn2_w_up, ffn2_w_down, ple_w_proj, ple_w_gate, ple_norm_g):
    batch, seq, d = x.shape
    depth = p.shape[0]
    assert d == D_MODEL and seq % TM == 0
    t = batch * seq
    h = x.reshape(t, d)
    pos = positions.reshape(t, 1)
    invf, sgn, dintra, qdec, kdec, cdec = _tables()
    bf = lambda w: w.astype(BF16)
    for i in range(depth):
        h1, z, rg, (wg2, wu2, wd2, wo, wpg) = _ffn1_proj_call(
            h, pos, ffn1_pre_g[i][None], ffn1_post_g[i][None],
            bf(ffn1_w_gate[i]), bf(ffn1_w_up[i]), bf(ffn1_w_down[i]),
            mix_pre_g[i][None], bf(w_in[i]), b_in[i][None], invf, sgn, qdec, kdec,
            to_cast=(ffn2_w_gate[i], ffn2_w_up[i], ffn2_w_down[i], w_out[i], ple_w_gate[i]))
        h = _mix_ffn2_call(
            h1, z, rg, dintra, cdec, swa_sinks[i], wo, mix_post_g[i][None],
            ffn2_pre_g[i][None], ffn2_post_g[i][None], wg2, wu2, wd2,
            p[i].reshape(t, PLE_DIM), wpg, bf(ple_w_proj[i]), ple_norm_g[i][None], seq=seq)
    return h.reshape(batch, seq, d)
```

```python
import functools

import jax
import jax.numpy as jnp
import numpy as np
from jax import lax
from jax.experimental import pallas as pl
from jax.experimental.pallas import tpu as pltpu

D_MODEL = 1024
PLE_DIM = 256
D_FF = 2816
RET_HEADS = 4
RET_QK_DIM = 64
RET_V_DIM = 128
CHUNK = 128
SWA_Q_HEADS = 8
SWA_KV_HEADS = 2
SWA_HEAD_DIM = 64
ROPE_BASE = 10000.0
EPS = 1e-6
NEG_INF = -1e30

RET_QK_W = RET_HEADS * RET_QK_DIM
RET_V_W = RET_HEADS * RET_V_DIM
SWA_Q_W = SWA_Q_HEADS * SWA_HEAD_DIM
SWA_KV_W = SWA_KV_HEADS * SWA_HEAD_DIM
IN_W = 2 * RET_QK_W + 2 * RET_V_W + SWA_Q_W + 2 * SWA_KV_W
OFF_RQ = 0
OFF_RK = OFF_RQ + RET_QK_W
OFF_RV = OFF_RK + RET_QK_W
OFF_RG = OFF_RV + RET_V_W
OFF_SQ = OFF_RG + RET_V_W
OFF_SK = OFF_SQ + SWA_Q_W
OFF_SV = OFF_SK + SWA_KV_W
Z_RQ = 0
Z_RQD = Z_RQ + RET_QK_W
Z_RK = Z_RQD + RET_QK_W
Z_RKD = Z_RK + RET_QK_W
Z_RV = Z_RKD + RET_QK_W
Z_SQ = Z_RV + RET_V_W
Z_SK = Z_SQ + SWA_Q_W
Z_SKX = Z_SK + SWA_KV_W
Z_SV = Z_SKX + SWA_KV_W
Z_SVX = Z_SV + SWA_KV_W
Z_W = Z_SVX + SWA_KV_W

LANES = 128
BF16_ROWS = 16
FF_EDGES = tuple(range(0, D_FF + 1, 256))
TM = 512
TRIG_ROWS = 64
FFN_SPLIT = 5
VMEM_LIMIT = 60 * 1024 * 1024

BF16 = jnp.bfloat16
F32 = jnp.float32


def _rms(x, g):
    return x * lax.rsqrt(jnp.mean(x * x, axis=-1, keepdims=True) + EPS) * g


def _const_spec(shape):
    nd = len(shape)
    return pl.BlockSpec(shape, lambda *_: (0,) * nd, pipeline_mode=pl.Buffered(1))


def _gate_up(a, wg_ref, wu_ref, c):
    cols = slice(FF_EDGES[c], FF_EDGES[c + 1])
    g = jnp.dot(a, wg_ref[:, cols], preferred_element_type=F32).astype(BF16)
    u = jnp.dot(a, wu_ref[:, cols], preferred_element_type=F32).astype(BF16)
    return g * jax.nn.sigmoid(g) * u


def _down(acc_ref, hm, wd_ref, c):
    d = jnp.dot(hm, wd_ref[FF_EDGES[c]:FF_EDGES[c + 1], :], preferred_element_type=F32)
    if c == 0:
        acc_ref[...] = d
    else:
        acc_ref[...] += d


def _ffn_slices(acc_ref, a, wg_ref, wu_ref, wd_ref, lo, hi, between):
    hm_next = _gate_up(a, wg_ref, wu_ref, lo)
    for c in range(lo, hi):
        hm = hm_next
        if c in between:
            between[c]()
        if c + 1 < hi:
            hm_next = _gate_up(a, wg_ref, wu_ref, c + 1)
        _down(acc_ref, hm, wd_ref, c)


def _ffn1_proj_kernel(x_ref, pos_ref, pre_g_ref, post_g_ref, wg_ref, wu_ref, wd_ref,
                      mix_g_ref, w_in_ref, b_in_ref, invf_ref, sgn_ref, qdec_ref, kdec_ref, *rest, n_cast):
    cast_in, outs = rest[:n_cast], rest[n_cast:n_cast + 3]
    cast_out, scratch = rest[n_cast + 3:2 * n_cast + 3], rest[2 * n_cast + 3:]

    for src_ref, dst_ref in zip(cast_in, cast_out):
        dst_ref[...] = src_ref[...].astype(BF16)

    refs = (x_ref, pos_ref, pre_g_ref, post_g_ref, wg_ref, wu_ref, wd_ref, mix_g_ref, w_in_ref, b_in_ref,
            invf_ref, sgn_ref, qdec_ref, kdec_ref, *outs, *scratch)
    step = pl.program_id(0)
    last = pl.num_programs(0) - 1
    pl.when(step == 0)(functools.partial(_ffn1_proj_step, True, False, *refs))
    pl.when(jnp.logical_and(step > 0, step < last))(functools.partial(_ffn1_proj_step, True, True, *refs))
    pl.when(step == last)(functools.partial(_ffn1_proj_step, False, True, *refs))


def _ffn1_proj_step(with_ffn, with_proj, x_ref, pos_ref, pre_g_ref, post_g_ref, wg_ref, wu_ref, wd_ref,
                    mix_g_ref, w_in_ref, b_in_ref, invf_ref, sgn_ref, qdec_ref, kdec_ref,
                    h_ref, z_ref, rg_ref, acc_ref, u_ref, trig_ref):
    u = u_ref[...] if with_proj else None

    def store_trig(piece):
        rows = slice(piece * TRIG_ROWS, (piece + 1) * TRIG_ROWS)
        ang = pos_ref[rows, :].astype(F32) * invf_ref[...]
        trig_ref[0, rows, :] = jnp.cos(ang)
        trig_ref[1, rows, :] = jnp.sin(ang) * sgn_ref[...]

    def proj(off, width):
        cols = slice(off, off + width)
        return jnp.dot(u, w_in_ref[:, cols], preferred_element_type=F32) + b_in_ref[:, cols]

    def store_rotary():
        rqk = proj(OFF_RQ, 2 * RET_QK_W)
        cos, sin = trig_ref[0], trig_ref[1]
        lane = lax.broadcasted_iota(jnp.int32, (TM, LANES), 1)
        lo_half = (lane % RET_QK_DIM) < (RET_QK_DIM // 2)

        def rotary(v):
            swapped = jnp.where(lo_half, pltpu.roll(v, LANES - RET_QK_DIM // 2, 1),
                                pltpu.roll(v, RET_QK_DIM // 2, 1))
            return v * cos + swapped * sin

        for col in range(RET_QK_W // LANES):
            q = rotary(rqk[:, col * LANES:(col + 1) * LANES])
            k = rotary(rqk[:, RET_QK_W + col * LANES:RET_QK_W + (col + 1) * LANES]) * (RET_QK_DIM ** -0.5)
            cols = slice(col * LANES, (col + 1) * LANES)
            z_ref[:, Z_RQ + col * LANES:Z_RQ + (col + 1) * LANES] = q.astype(BF16)
            z_ref[:, Z_RK + col * LANES:Z_RK + (col + 1) * LANES] = k.astype(BF16)
            z_ref[:, Z_RQD + col * LANES:Z_RQD + (col + 1) * LANES] = (q * qdec_ref[:, cols]).astype(BF16)
            z_ref[:, Z_RKD + col * LANES:Z_RKD + (col + 1) * LANES] = (k * kdec_ref[:, cols]).astype(BF16)

    def store_swa_kv():
        skv = proj(OFF_SK, 2 * SWA_KV_W)
        z_ref[:, Z_SK:Z_SK + SWA_KV_W] = skv[:, :SWA_KV_W].astype(BF16)
        z_ref[:, Z_SV:Z_SV + SWA_KV_W] = skv[:, SWA_KV_W:].astype(BF16)
        z_ref[:, Z_SKX:Z_SKX + SWA_KV_W] = pltpu.roll(skv[:, :SWA_KV_W], SWA_HEAD_DIM, 1).astype(BF16)
        z_ref[:, Z_SVX:Z_SVX + SWA_KV_W] = pltpu.roll(skv[:, SWA_KV_W:], SWA_HEAD_DIM, 1).astype(BF16)

    def store_swa_q():
        z_ref[:, Z_SQ:Z_SQ + SWA_Q_W] = (proj(OFF_SQ, SWA_Q_W) * (SWA_HEAD_DIM ** -0.5)).astype(BF16)

    def store_ret_v():
        z_ref[:, Z_RV:Z_RV + RET_V_W] = proj(OFF_RV, RET_V_W).astype(BF16)

    n_trig = TM // TRIG_ROWS
    n_ffn = len(FF_EDGES) - 1
    assert n_trig + 1 < n_ffn
    extra = {1: store_swa_kv, 3: store_swa_q, n_trig: store_rotary}

    def after_slice(c):
        if c < n_trig:
            store_trig(c)
        if c in extra:
            extra[c]()

    hooks = {c: functools.partial(after_slice, c) for c in range(n_trig + 1)} if with_proj else {}
    if with_proj:
        rg_ref[...] = proj(OFF_RG, RET_V_W)
    if with_ffn:
        x = x_ref[...]
        a = _rms(x, pre_g_ref[...]).astype(BF16)
        _ffn_slices(acc_ref, a, wg_ref, wu_ref, wd_ref, 0, n_ffn, hooks)
    else:
        for c in sorted(hooks):
            hooks[c]()
    if with_proj:
        store_ret_v()
    if with_ffn:
        h = x + 0.5 * _rms(acc_ref[...], post_g_ref[...])
        h_ref[...] = h
        u_ref[...] = _rms(h, mix_g_ref[...]).astype(BF16)


def _cast_spec(shape, n_steps):
    rows, cols = shape
    block = next(b for b in range(BF16_ROWS, rows + 1, BF16_ROWS) if rows % b == 0 and rows // b <= n_steps)
    return pl.BlockSpec((block, cols), lambda i: (jnp.minimum(i, rows // block - 1), 0))


def _ffn1_proj_call(x, pos, pre_g, post_g, wg, wu, wd, mix_g, w_in, b_in, invf, sgn, qdec, kdec, to_cast):
    t, d = x.shape
    n_tiles = t // TM
    cur = lambda w: pl.BlockSpec((TM, w), lambda i: (jnp.minimum(i, n_tiles - 1), 0))
    prev = lambda w: pl.BlockSpec((TM, w), lambda i: (jnp.maximum(i - 1, 0), 0))
    cast_specs = [_cast_spec(w.shape, n_tiles) for w in to_cast]
    in_specs = [cur(d), prev(1), _const_spec((1, d)), _const_spec((1, d)),
                _const_spec((d, D_FF)), _const_spec((d, D_FF)), _const_spec((D_FF, d)),
                _const_spec((1, d)), _const_spec((d, IN_W)), _const_spec((1, IN_W)),
                _const_spec((1, LANES)), _const_spec((1, LANES)),
                _const_spec((TM, RET_QK_W)), _const_spec((TM, RET_QK_W))] + cast_specs
    h1, z, rg, *cast = pl.pallas_call(
        functools.partial(_ffn1_proj_kernel, n_cast=len(to_cast)),
        grid=(n_tiles + 1,),
        in_specs=in_specs,
        out_specs=[cur(d), prev(Z_W), prev(RET_V_W)] + cast_specs,
        out_shape=[jax.ShapeDtypeStruct((t, d), F32), jax.ShapeDtypeStruct((t, Z_W), BF16),
                   jax.ShapeDtypeStruct((t, RET_V_W), F32)]
                  + [jax.ShapeDtypeStruct(w.shape, BF16) for w in to_cast],
        scratch_shapes=[pltpu.VMEM((TM, d), F32),
                        pltpu.VMEM((TM, d), BF16),
                        pltpu.VMEM((2, TM, LANES), F32)],
        compiler_params=pltpu.CompilerParams(
            dimension_semantics=("arbitrary",), vmem_limit_bytes=VMEM_LIMIT),
        name="ffn1_proj",
    )(x, pos, pre_g, post_g, wg, wu, wd, mix_g, w_in, b_in, invf, sgn, qdec, kdec, *to_cast)
    return h1, z, rg, cast


class _Attention:
    def __init__(self, z_ref, rg_ref, dintra_ref, cdec_ref, sink_ref, mixin_ref, state_ref, kprev_ref, vprev_ref,
                 seq_tile):
        self.z_ref, self.rg_ref, self.dintra_ref, self.cdec_ref = z_ref, rg_ref, dintra_ref, cdec_ref
        self.sink_ref, self.mixin_ref, self.state_ref = sink_ref, mixin_ref, state_ref
        self.kprev_ref, self.vprev_ref, self.seq_tile = kprev_ref, vprev_ref, seq_tile
        lane = lax.broadcasted_iota(jnp.int32, (CHUNK, LANES), 1)
        self.lo_head = lane < RET_QK_DIM
        self.lo256 = lax.broadcasted_iota(jnp.int32, (2 * CHUNK, LANES), 1) < SWA_HEAD_DIM
        row = lax.broadcasted_iota(jnp.int32, (2 * CHUNK, 2 * CHUNK), 0)
        qi = row % CHUNK
        self.kj = lax.broadcasted_iota(jnp.int32, (2 * CHUNK, 2 * CHUNK), 1)
        self.band = (self.kj > qi) & (self.kj <= qi + CHUNK)
        self.top_rows = lax.broadcasted_iota(jnp.int32, (2 * CHUNK, 1), 0) < CHUNK

    def scores(self, c):
        z_ref = self.z_ref
        rows = slice(c * CHUNK, (c + 1) * CHUNK)
        zero = jnp.zeros((), BF16)
        nt = (((1,), (1,)), ((), ()))
        ret = []
        for col in range(RET_QK_W // LANES):
            q2 = z_ref[rows, Z_RQ + col * LANES:Z_RQ + (col + 1) * LANES]
            k2 = z_ref[rows, Z_RK + col * LANES:Z_RK + (col + 1) * LANES]
            k2d = z_ref[rows, Z_RKD + col * LANES:Z_RKD + (col + 1) * LANES]
            for sub in range(2):
                head = 2 * col + sub
                keep = self.lo_head if sub == 0 else ~self.lo_head
                kh = jnp.where(keep, k2, zero)
                khd = jnp.where(keep, k2d, zero)
                v = z_ref[rows, Z_RV + head * RET_V_DIM:Z_RV + (head + 1) * RET_V_DIM]
                s = lax.dot_general(q2, kh, nt, preferred_element_type=F32)
                kv = lax.dot_general(khd, v, (((0,), (0,)), ((), ())), preferred_element_type=F32)
                ret.append((s, kv, v))

        def window(off, prev_ref, lane_off):
            own = z_ref[rows, off:off + SWA_KV_W]
            if c == 0:
                prev = prev_ref[:, lane_off:lane_off + SWA_KV_W]
            else:
                prev = z_ref[(c - 1) * CHUNK:c * CHUNK, off:off + SWA_KV_W]
            return jnp.concatenate([prev, own], axis=0)

        kw, kwx = window(Z_SK, self.kprev_ref, 0), window(Z_SKX, self.kprev_ref, SWA_KV_W)
        vw, vwx = window(Z_SV, self.vprev_ref, 0), window(Z_SVX, self.vprev_ref, SWA_KV_W)
        swa = []
        for j in range(SWA_KV_HEADS):
            k_a, k_b = (kw, kwx) if j == 0 else (kwx, kw)
            v_a, v_b = (vw, vwx) if j == 0 else (vwx, vw)
            ks = (jnp.where(self.lo256, k_a, zero), jnp.where(self.lo256, zero, k_b))
            vs = (jnp.where(self.lo256, v_a, zero), jnp.where(self.lo256, zero, v_b))
            qcol = Z_SQ + 2 * j * LANES
            q4 = jnp.concatenate([z_ref[rows, qcol:qcol + LANES], z_ref[rows, qcol + LANES:qcol + 2 * LANES]],
                                 axis=0)
            sc = [lax.dot_general(q4, kk, nt, preferred_element_type=F32) for kk in ks]
            swa.append((sc, vs))
        return ret, swa

    def finish(self, c, staged):
        ret, swa = staged
        rows = slice(c * CHUNK, (c + 1) * CHUNK)
        for head, (s, kv, v) in enumerate(ret):
            col = head // 2
            q2d = self.z_ref[rows, Z_RQD + col * LANES:Z_RQD + (col + 1) * LANES]
            s = (s * self.dintra_ref[head]).astype(BF16)
            st = self.state_ref[head]
            y = (jnp.dot(s, v, preferred_element_type=F32)
                 + jnp.dot(q2d, st.astype(BF16), preferred_element_type=F32))
            self.state_ref[head] = self.cdec_ref[head] * st + kv
            mu = jnp.mean(y, axis=-1, keepdims=True)
            yc = y - mu
            var = jnp.mean(yc * yc, axis=-1, keepdims=True)
            yn = yc * lax.rsqrt(var + EPS)
            gte = self.rg_ref[rows, head * RET_V_DIM:(head + 1) * RET_V_DIM]
            self.mixin_ref[rows, head * RET_V_DIM:(head + 1) * RET_V_DIM] = (
                gte * jax.nn.sigmoid(gte) * yn).astype(BF16)

        first = jnp.logical_and(self.seq_tile == 0, c == 0)
        key_lo = jnp.where(first, CHUNK, 0)
        valid = self.band & (self.kj >= key_lo)
        for j, (scs, vs) in enumerate(swa):
            outs, invs = [], []
            for sub, (sc, vv) in enumerate(zip(scs, vs)):
                sc = jnp.where(valid, sc, NEG_INF)
                sink = jnp.where(self.top_rows, self.sink_ref[4 * j + sub], self.sink_ref[4 * j + 2 + sub])
                m = jnp.maximum(jnp.max(sc, axis=-1, keepdims=True), sink)
                e = jnp.exp(sc - m)
                den = jnp.sum(e, axis=-1, keepdims=True) + jnp.exp(sink - m)
                outs.append(jnp.dot(e.astype(BF16), vv, preferred_element_type=F32))
                invs.append(1.0 / den)
            o4 = (outs[0] + outs[1]) * jnp.where(self.lo256, invs[0], invs[1])
            for col in range(2):
                ocol = RET_V_W + (2 * j + col) * LANES
                self.mixin_ref[rows, ocol:ocol + LANES] = o4[col * CHUNK:(col + 1) * CHUNK].astype(BF16)

    def carry(self):
        last = slice(TM - CHUNK, TM)
        z_ref = self.z_ref
        self.kprev_ref[:, :SWA_KV_W] = z_ref[last, Z_SK:Z_SK + SWA_KV_W]
        self.kprev_ref[:, SWA_KV_W:] = z_ref[last, Z_SKX:Z_SKX + SWA_KV_W]
        self.vprev_ref[:, :SWA_KV_W] = z_ref[last, Z_SV:Z_SV + SWA_KV_W]
        self.vprev_ref[:, SWA_KV_W:] = z_ref[last, Z_SVX:Z_SVX + SWA_KV_W]


def _mix_ffn2_kernel(h1_ref, z_ref, rg_ref, dintra_ref, cdec_ref, sink_ref, w_out_ref, mix_g_ref,
                     pre_g_ref, post_g_ref, wg_ref, wu_ref, wd_ref, p_ref, wpg_ref, wpp_ref, ple_g_ref,
                     o_ref, h2_ref, a_ref, acc_ref, mixin_ref, state_ref, kprev_ref, vprev_ref, *, tiles_per_seq):
    step = pl.program_id(0)
    last = pl.num_programs(0) - 1
    seq_tile = step % tiles_per_seq

    @pl.when(seq_tile == 0)
    def _():
        state_ref[...] = jnp.zeros_like(state_ref)
        kprev_ref[...] = jnp.zeros_like(kprev_ref)
        vprev_ref[...] = jnp.zeros_like(vprev_ref)

    refs = (h1_ref, z_ref, rg_ref, dintra_ref, cdec_ref, sink_ref, w_out_ref, mix_g_ref,
            pre_g_ref, post_g_ref, wg_ref, wu_ref, wd_ref, p_ref, wpg_ref, wpp_ref, ple_g_ref,
            o_ref, h2_ref, a_ref, acc_ref, mixin_ref, state_ref, kprev_ref, vprev_ref)
    pl.when(step == 0)(functools.partial(_mix_ffn2_step, False, True, seq_tile, *refs))
    pl.when(jnp.logical_and(step > 0, step < last))(functools.partial(_mix_ffn2_step, True, True, seq_tile, *refs))
    pl.when(step == last)(functools.partial(_mix_ffn2_step, True, False, seq_tile, *refs))


def _mix_ffn2_step(with_prev, with_cur, seq_tile, h1_ref, z_ref, rg_ref, dintra_ref, cdec_ref, sink_ref,
                   w_out_ref, mix_g_ref, pre_g_ref, post_g_ref, wg_ref, wu_ref, wd_ref, p_ref, wpg_ref, wpp_ref,
                   ple_g_ref, o_ref, h2_ref, a_ref, acc_ref, mixin_ref, state_ref, kprev_ref, vprev_ref):
    n_att = TM // CHUNK
    n_ffn = len(FF_EDGES) - 1
    assert n_att < n_ffn - FFN_SPLIT
    between = {}
    side = {}
    if with_cur:
        att = _Attention(z_ref, rg_ref, dintra_ref, cdec_ref, sink_ref, mixin_ref, state_ref, kprev_ref,
                         vprev_ref, seq_tile)
        staged = [att.scores(0)]

        def attention_chunk(k):
            att.finish(k, staged[0])
            if k + 1 < n_att:
                staged[0] = att.scores(k + 1)
            else:
                att.carry()

        between = {FFN_SPLIT + k: functools.partial(attention_chunk, k) for k in range(n_att)}

    if with_prev:
        def embed_proj():
            side["proj"] = jnp.dot(p_ref[...].astype(BF16), wpp_ref[...], preferred_element_type=F32)

        between[FFN_SPLIT + n_att] = embed_proj
        _ffn_slices(acc_ref, a_ref[...], wg_ref, wu_ref, wd_ref, FFN_SPLIT, n_ffn, between)
    else:
        for c in sorted(between):
            between[c]()

    if with_cur:
        mix = jnp.dot(mixin_ref[...], w_out_ref[...], preferred_element_type=F32)
    if with_prev:
        h = h2_ref[...] + 0.5 * _rms(acc_ref[...], post_g_ref[...])
        gate = jax.nn.sigmoid(jnp.dot(h.astype(BF16), wpg_ref[...], preferred_element_type=F32))

        def store_out():
            o_ref[...] = h + _rms(gate * side["proj"], ple_g_ref[...])

    if with_cur:
        h2 = h1_ref[...] + _rms(mix, mix_g_ref[...])
        h2_ref[...] = h2
        a = _rms(h2, pre_g_ref[...]).astype(BF16)
        a_ref[...] = a
        _ffn_slices(acc_ref, a, wg_ref, wu_ref, wd_ref, 0, FFN_SPLIT, {0: store_out} if with_prev else {})
    else:
        store_out()


def _mix_ffn2_call(h1, z, rg, dintra, cdec, sinks, w_out, mix_g, pre_g, post_g, wg, wu, wd,
                   p, wpg, wpp, ple_g, *, seq):
    t, d = h1.shape
    n_tiles = t // TM
    cur = lambda w: pl.BlockSpec((TM, w), lambda i: (jnp.minimum(i, n_tiles - 1), 0))
    prev = lambda w: pl.BlockSpec((TM, w), lambda i: (jnp.maximum(i - 1, 0), 0))
    in_specs = [cur(d), cur(Z_W), cur(RET_V_W),
                _const_spec(dintra.shape), _const_spec(cdec.shape),
                pl.BlockSpec(memory_space=pltpu.SMEM),
                _const_spec((d, d)), _const_spec((1, d)),
                _const_spec((1, d)), _const_spec((1, d)),
                _const_spec((d, D_FF)), _const_spec((d, D_FF)), _const_spec((D_FF, d)),
                prev(PLE_DIM), _const_spec((d, d)), _const_spec((PLE_DIM, d)), _const_spec((1, d))]
    return pl.pallas_call(
        functools.partial(_mix_ffn2_kernel, tiles_per_seq=seq // TM),
        grid=(n_tiles + 1,),
        in_specs=in_specs,
        out_specs=prev(d),
        out_shape=jax.ShapeDtypeStruct((t, d), F32),
        scratch_shapes=[
            pltpu.VMEM((TM, d), F32),
            pltpu.VMEM((TM, d), BF16),
            pltpu.VMEM((TM, d), F32),
            pltpu.VMEM((TM, d), BF16),
            pltpu.VMEM((RET_HEADS, CHUNK, RET_V_DIM), F32),
            pltpu.VMEM((CHUNK, 2 * SWA_KV_W), BF16),
            pltpu.VMEM((CHUNK, 2 * SWA_KV_W), BF16),
        ],
        compiler_params=pltpu.CompilerParams(
            dimension_semantics=("arbitrary",), vmem_limit_bytes=VMEM_LIMIT),
        name="mix_ffn2",
    )(h1, z, rg, dintra, cdec, sinks, w_out, mix_g, pre_g, post_g, wg, wu, wd, p, wpg, wpp, ple_g)


def _tables():
    f32 = np.float32
    half = RET_QK_DIM // 2
    inv_freq = f32(ROPE_BASE) ** (-np.arange(half, dtype=f32) / f32(half))
    invf = np.tile(inv_freq, LANES // half)[None, :]
    sgn = np.tile(np.concatenate([-np.ones(half, f32), np.ones(half, f32)]), LANES // RET_QK_DIM)[None, :]
    log_gamma = np.log(f32(1.0) - f32(2.0) ** (f32(-5.0) - np.arange(RET_HEADS, dtype=f32)))
    idx = np.arange(CHUNK, dtype=f32)
    diff = idx[:, None] - idx[None, :]
    causal = diff >= 0
    dintra = np.where(causal[None], np.exp(log_gamma[:, None, None] * np.where(causal, diff, f32(0.0))[None]),
                      f32(0.0)).astype(f32)
    q_decay = np.exp(log_gamma[:, None] * (idx[None, :] + f32(1.0)))
    k_decay = np.exp(log_gamma[:, None] * (f32(CHUNK - 1.0) - idx[None, :]))
    chunk_decay = np.exp(log_gamma * f32(CHUNK))

    def per_token(dec):
        x = np.repeat(np.transpose(dec)[:, :, None], RET_QK_DIM, axis=2).reshape(CHUNK, RET_QK_W)
        return np.tile(x, (TM // CHUNK, 1)).astype(f32)

    cdec = np.broadcast_to(chunk_decay[:, None, None], (RET_HEADS, CHUNK, RET_V_DIM)).astype(f32)
    return invf.astype(f32), sgn, dintra, per_token(q_decay), per_token(k_decay), cdec


def kernel(x, p, positions, ffn1_pre_g, ffn1_post_g, ffn1_w_gate, ffn1_w_up, ffn1_w_down, mix_pre_g, mix_post_g, w_in, b_in, swa_sinks, w_out, ffn2_pre_g, ffn2_post_g, ffn2_w_gate, ffn2_w_up, ffn2_w_down, ple_w_proj, ple_w_gate, ple_norm_g):
    batch, seq, d = x.shape
    depth = p.shape[0]
    assert d == D_MODEL and seq % TM == 0
    t = batch * seq
    h = x.reshape(t, d)
    pos = positions.reshape(t, 1)
    invf, sgn, dintra, qdec, kdec, cdec = _tables()
    bf = lambda w: w.astype(BF16)
    for i in range(depth):
        h1, z, rg, (wg2, wu2, wd2, wo, wpg) = _ffn1_proj_call(
            h, pos, ffn1_pre_g[i][None], ffn1_post_g[i][None],
            bf(ffn1_w_gate[i]), bf(ffn1_w_up[i]), bf(ffn1_w_down[i]),
            mix_pre_g[i][None], bf(w_in[i]), b_in[i][None], invf, sgn, qdec, kdec,
            to_cast=(ffn2_w_gate[i], ffn2_w_up[i], ffn2_w_down[i], w_out[i], ple_w_gate[i]))
        h = _mix_ffn2_call(
            h1, z, rg, dintra, cdec, swa_sinks[i], wo, mix_post_g[i][None],
            ffn2_pre_g[i][None], ffn2_post_g[i][None], wg2, wu2, wd2,
            p[i].reshape(t, PLE_DIM), wpg, bf(ple_w_proj[i]), ple_norm_g[i][None], seq=seq)
    return h.reshape(batch, seq, d)
```

```python
import functools

import jax
import jax.numpy as jnp
import numpy as np
from jax import lax
from jax.experimental import pallas as pl
from jax.experimental.pallas import tpu as pltpu

D_MODEL = 1024
PLE_DIM = 256
D_FF = 2816
RET_HEADS = 4
RET_QK_DIM = 64
RET_V_DIM = 128
CHUNK = 128
SWA_Q_HEADS = 8
SWA_KV_HEADS = 2
SWA_HEAD_DIM = 64
ROPE_BASE = 10000.0
EPS = 1e-6
NEG_INF = -1e30

RET_QK_W = RET_HEADS * RET_QK_DIM
RET_V_W = RET_HEADS * RET_V_DIM
SWA_Q_W = SWA_Q_HEADS * SWA_HEAD_DIM
SWA_KV_W = SWA_KV_HEADS * SWA_HEAD_DIM
IN_W = 2 * RET_QK_W + 2 * RET_V_W + SWA_Q_W + 2 * SWA_KV_W
OFF_RQ = 0
OFF_RK = OFF_RQ + RET_QK_W
OFF_RV = OFF_RK + RET_QK_W
OFF_RG = OFF_RV + RET_V_W
OFF_SQ = OFF_RG + RET_V_W
OFF_SK = OFF_SQ + SWA_Q_W
OFF_SV = OFF_SK + SWA_KV_W
Z_RQ = 0
Z_RQD = Z_RQ + RET_QK_W
Z_RK = Z_RQD + RET_QK_W
Z_RKD = Z_RK + RET_QK_W
Z_RV = Z_RKD + RET_QK_W
Z_SQ = Z_RV + RET_V_W
Z_SK = Z_SQ + SWA_Q_W
Z_SKX = Z_SK + SWA_KV_W
Z_SV = Z_SKX + SWA_KV_W
Z_SVX = Z_SV + SWA_KV_W
Z_W = Z_SVX + SWA_KV_W

LANES = 128
BF16_ROWS = 16
POS_ROWS = 8
FF_EDGES = tuple(range(0, D_FF + 1, 256))
TM = 512
TRIG_ROWS = 64
FFN_SPLIT = 5
VMEM_LIMIT = 60 * 1024 * 1024

BF16 = jnp.bfloat16
F32 = jnp.float32


def _rms(x, g):
    return x * lax.rsqrt(jnp.mean(x * x, axis=-1, keepdims=True) + EPS) * g


def _const_spec(shape):
    nd = len(shape)
    return pl.BlockSpec(shape, lambda *_: (0,) * nd, pipeline_mode=pl.Buffered(1))


def _gate_up(a, wg_ref, wu_ref, c):
    cols = slice(FF_EDGES[c], FF_EDGES[c + 1])
    g = jnp.dot(a, wg_ref[:, cols], preferred_element_type=F32).astype(BF16)
    u = jnp.dot(a, wu_ref[:, cols], preferred_element_type=F32).astype(BF16)
    return g * jax.nn.sigmoid(g) * u


def _down(acc_ref, hm, wd_ref, c):
    d = jnp.dot(hm, wd_ref[FF_EDGES[c]:FF_EDGES[c + 1], :], preferred_element_type=F32)
    if c == 0:
        acc_ref[...] = d
    else:
        acc_ref[...] += d


def _ffn_slices(acc_ref, a, wg_ref, wu_ref, wd_ref, lo, hi, between):
    hm_next = _gate_up(a, wg_ref, wu_ref, lo)
    for c in range(lo, hi):
        hm = hm_next
        if c in between:
            between[c]()
        if c + 1 < hi:
            hm_next = _gate_up(a, wg_ref, wu_ref, c + 1)
        _down(acc_ref, hm, wd_ref, c)


def _ffn1_proj_kernel(x_ref, pos_ref, pre_g_ref, post_g_ref, wg_ref, wu_ref, wd_ref,
                      mix_g_ref, w_in_ref, b_in_ref, invf_ref, sgn_ref, qdec_ref, kdec_ref, *rest, n_cast):
    cast_in, (h_ref, z_ref, rg_ref) = rest[:n_cast], rest[n_cast:n_cast + 3]
    cast_out, (acc_ref, u_ref, trig_ref) = rest[n_cast + 3:2 * n_cast + 3], rest[2 * n_cast + 3:]

    @pl.when(pl.program_id(0) == 0)
    def _():
        u_ref[...] = jnp.zeros_like(u_ref)

    for src_ref, dst_ref in zip(cast_in, cast_out):
        dst_ref[...] = src_ref[...].astype(BF16)

    u = u_ref[...]

    pos_t = jnp.transpose(pos_ref[...].astype(F32))
    chunks_per_tile = TM // CHUNK
    second_half = (jnp.maximum(pl.program_id(0) - 1, 0) % (POS_ROWS // chunks_per_tile)) == 1

    def store_trig(piece):
        rows = slice(piece * TRIG_ROWS, (piece + 1) * TRIG_ROWS)
        chunk, off = divmod(piece * TRIG_ROWS, CHUNK)
        pos = jnp.where(second_half, pos_t[:, chunks_per_tile + chunk:chunks_per_tile + chunk + 1],
                        pos_t[:, chunk:chunk + 1])[off:off + TRIG_ROWS]
        ang = pos * invf_ref[...]
        trig_ref[0, rows, :] = jnp.cos(ang)
        trig_ref[1, rows, :] = jnp.sin(ang) * sgn_ref[...]

    def proj(off, width):
        cols = slice(off, off + width)
        return jnp.dot(u, w_in_ref[:, cols], preferred_element_type=F32) + b_in_ref[:, cols]

    def store_rotary():
        rqk = proj(OFF_RQ, 2 * RET_QK_W)
        cos, sin = trig_ref[0], trig_ref[1]
        lane = lax.broadcasted_iota(jnp.int32, (TM, LANES), 1)
        lo_half = (lane % RET_QK_DIM) < (RET_QK_DIM // 2)

        def rotary(v):
            swapped = jnp.where(lo_half, pltpu.roll(v, LANES - RET_QK_DIM // 2, 1),
                                pltpu.roll(v, RET_QK_DIM // 2, 1))
            return v * cos + swapped * sin

        for col in range(RET_QK_W // LANES):
            q = rotary(rqk[:, col * LANES:(col + 1) * LANES])
            k = rotary(rqk[:, RET_QK_W + col * LANES:RET_QK_W + (col + 1) * LANES]) * (RET_QK_DIM ** -0.5)
            cols = slice(col * LANES, (col + 1) * LANES)
            z_ref[:, Z_RQ + col * LANES:Z_RQ + (col + 1) * LANES] = q.astype(BF16)
            z_ref[:, Z_RK + col * LANES:Z_RK + (col + 1) * LANES] = k.astype(BF16)
            z_ref[:, Z_RQD + col * LANES:Z_RQD + (col + 1) * LANES] = (q * qdec_ref[:, cols]).astype(BF16)
            z_ref[:, Z_RKD + col * LANES:Z_RKD + (col + 1) * LANES] = (k * kdec_ref[:, cols]).astype(BF16)

    def store_swa_kv():
        skv = proj(OFF_SK, 2 * SWA_KV_W)
        z_ref[:, Z_SK:Z_SK + SWA_KV_W] = skv[:, :SWA_KV_W].astype(BF16)
        z_ref[:, Z_SV:Z_SV + SWA_KV_W] = skv[:, SWA_KV_W:].astype(BF16)
        z_ref[:, Z_SKX:Z_SKX + SWA_KV_W] = pltpu.roll(skv[:, :SWA_KV_W], SWA_HEAD_DIM, 1).astype(BF16)
        z_ref[:, Z_SVX:Z_SVX + SWA_KV_W] = pltpu.roll(skv[:, SWA_KV_W:], SWA_HEAD_DIM, 1).astype(BF16)

    def store_swa_q():
        z_ref[:, Z_SQ:Z_SQ + SWA_Q_W] = (proj(OFF_SQ, SWA_Q_W) * (SWA_HEAD_DIM ** -0.5)).astype(BF16)

    def store_ret_v():
        z_ref[:, Z_RV:Z_RV + RET_V_W] = proj(OFF_RV, RET_V_W).astype(BF16)

    n_trig = TM // TRIG_ROWS
    n_ffn = len(FF_EDGES) - 1
    assert n_trig + 1 < n_ffn
    extra = {1: store_swa_kv, 3: store_swa_q, n_trig: store_rotary}

    def after_slice(c):
        if c < n_trig:
            store_trig(c)
        if c in extra:
            extra[c]()

    rg_ref[...] = proj(OFF_RG, RET_V_W)
    x = x_ref[...]
    a = _rms(x, pre_g_ref[...]).astype(BF16)
    _ffn_slices(acc_ref, a, wg_ref, wu_ref, wd_ref, 0, n_ffn,
                {c: functools.partial(after_slice, c) for c in range(n_trig + 1)})
    store_ret_v()
    h = x + 0.5 * _rms(acc_ref[...], post_g_ref[...])
    h_ref[...] = h
    u_ref[...] = _rms(h, mix_g_ref[...]).astype(BF16)


def _cast_spec(shape, n_steps):
    rows, cols = shape
    block = next(b for b in range(BF16_ROWS, rows + 1, BF16_ROWS) if rows % b == 0 and rows // b <= n_steps)
    return pl.BlockSpec((block, cols), lambda i: (jnp.minimum(i, rows // block - 1), 0))


def _ffn1_proj_call(x, pos, pre_g, post_g, wg, wu, wd, mix_g, w_in, b_in, invf, sgn, qdec, kdec, to_cast):
    t, d = x.shape
    n_tiles = t // TM
    cur = lambda w: pl.BlockSpec((TM, w), lambda i: (jnp.minimum(i, n_tiles - 1), 0))
    prev = lambda w: pl.BlockSpec((TM, w), lambda i: (jnp.maximum(i - 1, 0), 0))
    cast_specs = [_cast_spec(w.shape, n_tiles) for w in to_cast]
    tiles_per_pos_block = POS_ROWS * CHUNK // TM
    pos_spec = pl.BlockSpec((POS_ROWS, CHUNK), lambda i: (jnp.maximum(i - 1, 0) // tiles_per_pos_block, 0))
    in_specs = [cur(d), pos_spec, _const_spec((1, d)), _const_spec((1, d)),
                _const_spec((d, D_FF)), _const_spec((d, D_FF)), _const_spec((D_FF, d)),
                _const_spec((1, d)), _const_spec((d, IN_W)), _const_spec((1, IN_W)),
                _const_spec((1, LANES)), _const_spec((1, LANES)),
                _const_spec((TM, RET_QK_W)), _const_spec((TM, RET_QK_W))] + cast_specs
    h1, z, rg, *cast = pl.pallas_call(
        functools.partial(_ffn1_proj_kernel, n_cast=len(to_cast)),
        grid=(n_tiles + 1,),
        in_specs=in_specs,
        out_specs=[cur(d), prev(Z_W), prev(RET_V_W)] + cast_specs,
        out_shape=[jax.ShapeDtypeStruct((t, d), F32), jax.ShapeDtypeStruct((t, Z_W), BF16),
                   jax.ShapeDtypeStruct((t, RET_V_W), F32)]
                  + [jax.ShapeDtypeStruct(w.shape, BF16) for w in to_cast],
        scratch_shapes=[pltpu.VMEM((TM, d), F32),
                        pltpu.VMEM((TM, d), BF16),
                        pltpu.VMEM((2, TM, LANES), F32)],
        compiler_params=pltpu.CompilerParams(
            dimension_semantics=("arbitrary",), vmem_limit_bytes=VMEM_LIMIT),
        name="ffn1_proj",
    )(x, pos, pre_g, post_g, wg, wu, wd, mix_g, w_in, b_in, invf, sgn, qdec, kdec, *to_cast)
    return h1, z, rg, cast


class _Attention:
    def __init__(self, z_ref, rg_ref, dintra_ref, cdec_ref, mask_ref, sink_ref, mixin_ref, state_ref, kprev_ref,
                 vprev_ref, seq_tile):
        self.z_ref, self.rg_ref, self.dintra_ref, self.cdec_ref = z_ref, rg_ref, dintra_ref, cdec_ref
        self.mask_ref, self.sink_ref, self.mixin_ref, self.state_ref = mask_ref, sink_ref, mixin_ref, state_ref
        self.kprev_ref, self.vprev_ref, self.seq_tile = kprev_ref, vprev_ref, seq_tile
        lane = lax.broadcasted_iota(jnp.int32, (CHUNK, LANES), 1)
        self.lo_head = lane < RET_QK_DIM
        self.lo256 = lax.broadcasted_iota(jnp.int32, (2 * CHUNK, LANES), 1) < SWA_HEAD_DIM
        self.top_rows = lax.broadcasted_iota(jnp.int32, (2 * CHUNK, 1), 0) < CHUNK

    def scores(self, c):
        z_ref = self.z_ref
        rows = slice(c * CHUNK, (c + 1) * CHUNK)
        zero = jnp.zeros((), BF16)
        nt = (((1,), (1,)), ((), ()))
        ret = []
        for col in range(RET_QK_W // LANES):
            q2 = z_ref[rows, Z_RQ + col * LANES:Z_RQ + (col + 1) * LANES]
            k2 = z_ref[rows, Z_RK + col * LANES:Z_RK + (col + 1) * LANES]
            k2d = z_ref[rows, Z_RKD + col * LANES:Z_RKD + (col + 1) * LANES]
            for sub in range(2):
                head = 2 * col + sub
                keep = self.lo_head if sub == 0 else ~self.lo_head
                kh = jnp.where(keep, k2, zero)
                khd = jnp.where(keep, k2d, zero)
                v = z_ref[rows, Z_RV + head * RET_V_DIM:Z_RV + (head + 1) * RET_V_DIM]
                s = lax.dot_general(q2, kh, nt, preferred_element_type=F32)
                kv = lax.dot_general(khd, v, (((0,), (0,)), ((), ())), preferred_element_type=F32)
                ret.append((s, kv, v))

        def window(off, prev_ref, lane_off):
            own = z_ref[rows, off:off + SWA_KV_W]
            if c == 0:
                prev = prev_ref[:, lane_off:lane_off + SWA_KV_W]
            else:
                prev = z_ref[(c - 1) * CHUNK:c * CHUNK, off:off + SWA_KV_W]
            return jnp.concatenate([prev, own], axis=0)

        kw, kwx = window(Z_SK, self.kprev_ref, 0), window(Z_SKX, self.kprev_ref, SWA_KV_W)
        vw, vwx = window(Z_SV, self.vprev_ref, 0), window(Z_SVX, self.vprev_ref, SWA_KV_W)
        swa = []
        for j in range(SWA_KV_HEADS):
            k_a, k_b = (kw, kwx) if j == 0 else (kwx, kw)
            v_a, v_b = (vw, vwx) if j == 0 else (vwx, vw)
            ks = (jnp.where(self.lo256, k_a, zero), jnp.where(self.lo256, zero, k_b))
            vs = (jnp.where(self.lo256, v_a, zero), jnp.where(self.lo256, zero, v_b))
            qcol = Z_SQ + 2 * j * LANES
            q4 = jnp.concatenate([z_ref[rows, qcol:qcol + LANES], z_ref[rows, qcol + LANES:qcol + 2 * LANES]],
                                 axis=0)
            sc = [lax.dot_general(q4, kk, nt, preferred_element_type=F32) for kk in ks]
            swa.append((sc, vs))
        return ret, swa

    def finish(self, c, staged):
        ret, swa = staged
        rows = slice(c * CHUNK, (c + 1) * CHUNK)
        for head, (s, kv, v) in enumerate(ret):
            col = head // 2
            q2d = self.z_ref[rows, Z_RQD + col * LANES:Z_RQD + (col + 1) * LANES]
            s = (s * self.dintra_ref[head]).astype(BF16)
            st = self.state_ref[head]
            y = (jnp.dot(s, v, preferred_element_type=F32)
                 + jnp.dot(q2d, st.astype(BF16), preferred_element_type=F32))
            self.state_ref[head] = self.cdec_ref[head] * st + kv
            mu = jnp.mean(y, axis=-1, keepdims=True)
            yc = y - mu
            var = jnp.mean(yc * yc, axis=-1, keepdims=True)
            yn = yc * lax.rsqrt(var + EPS)
            gte = self.rg_ref[rows, head * RET_V_DIM:(head + 1) * RET_V_DIM]
            self.mixin_ref[rows, head * RET_V_DIM:(head + 1) * RET_V_DIM] = (
                gte * jax.nn.sigmoid(gte) * yn).astype(BF16)

        visible = (self.mask_ref[jnp.where(self.seq_tile == 0, 1, 0)] if c == 0 else self.mask_ref[0]) != 0.0
        for j, (scs, vs) in enumerate(swa):
            outs, invs = [], []
            for sub, (sc, vv) in enumerate(zip(scs, vs)):
                sc = jnp.where(visible, sc, NEG_INF)
                sink = jnp.where(self.top_rows, self.sink_ref[4 * j + sub], self.sink_ref[4 * j + 2 + sub])
                m = jnp.maximum(jnp.max(sc, axis=-1, keepdims=True), sink)
                e = jnp.exp(sc - m)
                den = jnp.sum(e, axis=-1, keepdims=True) + jnp.exp(sink - m)
                outs.append(jnp.dot(e.astype(BF16), vv, preferred_element_type=F32))
                invs.append(1.0 / den)
            o4 = (outs[0] + outs[1]) * jnp.where(self.lo256, invs[0], invs[1])
            for col in range(2):
                ocol = RET_V_W + (2 * j + col) * LANES
                self.mixin_ref[rows, ocol:ocol + LANES] = o4[col * CHUNK:(col + 1) * CHUNK].astype(BF16)

    def carry(self):
        last = slice(TM - CHUNK, TM)
        z_ref = self.z_ref
        self.kprev_ref[:, :SWA_KV_W] = z_ref[last, Z_SK:Z_SK + SWA_KV_W]
        self.kprev_ref[:, SWA_KV_W:] = z_ref[last, Z_SKX:Z_SKX + SWA_KV_W]
        self.vprev_ref[:, :SWA_KV_W] = z_ref[last, Z_SV:Z_SV + SWA_KV_W]
        self.vprev_ref[:, SWA_KV_W:] = z_ref[last, Z_SVX:Z_SVX + SWA_KV_W]


def _mix_ffn2_kernel(h1_ref, z_ref, rg_ref, dintra_ref, cdec_ref, mask_ref, sink_ref, w_out_ref, mix_g_ref,
                     pre_g_ref, post_g_ref, wg_ref, wu_ref, wd_ref, p_ref, wpg_ref, wpp_ref, ple_g_ref,
                     o_ref, h2_ref, a_ref, acc_ref, mixin_ref, state_ref, kprev_ref, vprev_ref, *, tiles_per_seq):
    step = pl.program_id(0)
    n_tiles = pl.num_programs(0) - 1
    seq_tile = jnp.minimum(step, n_tiles - 1) % tiles_per_seq

    @pl.when(step == 0)
    def _():
        h2_ref[...] = jnp.zeros_like(h2_ref)
        a_ref[...] = jnp.zeros_like(a_ref)
        acc_ref[...] = jnp.zeros_like(acc_ref)

    @pl.when(seq_tile == 0)
    def _():
        state_ref[...] = jnp.zeros_like(state_ref)
        kprev_ref[...] = jnp.zeros_like(kprev_ref)
        vprev_ref[...] = jnp.zeros_like(vprev_ref)

    att = _Attention(z_ref, rg_ref, dintra_ref, cdec_ref, mask_ref, sink_ref, mixin_ref, state_ref, kprev_ref,
                     vprev_ref, seq_tile)
    n_att = TM // CHUNK
    n_ffn = len(FF_EDGES) - 1
    assert n_att < n_ffn - FFN_SPLIT
    staged = [att.scores(0)]
    side = {}

    def attention_chunk(k):
        att.finish(k, staged[0])
        if k + 1 < n_att:
            staged[0] = att.scores(k + 1)

    def after_attention():
        att.carry()
        side["proj"] = jnp.dot(p_ref[...].astype(BF16), wpp_ref[...], preferred_element_type=F32)

    between = {FFN_SPLIT + k: functools.partial(attention_chunk, k) for k in range(n_att)}
    between[FFN_SPLIT + n_att] = after_attention
    _ffn_slices(acc_ref, a_ref[...], wg_ref, wu_ref, wd_ref, FFN_SPLIT, n_ffn, between)

    mix = jnp.dot(mixin_ref[...], w_out_ref[...], preferred_element_type=F32)
    h = h2_ref[...] + 0.5 * _rms(acc_ref[...], post_g_ref[...])
    gate = jax.nn.sigmoid(jnp.dot(h.astype(BF16), wpg_ref[...], preferred_element_type=F32))
    h2 = h1_ref[...] + _rms(mix, mix_g_ref[...])
    h2_ref[...] = h2
    a = _rms(h2, pre_g_ref[...]).astype(BF16)
    a_ref[...] = a

    def store_out():
        o_ref[...] = h + _rms(gate * side["proj"], ple_g_ref[...])

    _ffn_slices(acc_ref, a, wg_ref, wu_ref, wd_ref, 0, FFN_SPLIT, {0: store_out})


def _mix_ffn2_call(h1, z, rg, dintra, cdec, swa_mask, sinks, w_out, mix_g, pre_g, post_g, wg, wu, wd,
                   p, wpg, wpp, ple_g, *, seq):
    t, d = h1.shape
    n_tiles = t // TM
    cur = lambda w: pl.BlockSpec((TM, w), lambda i: (jnp.minimum(i, n_tiles - 1), 0))
    prev = lambda w: pl.BlockSpec((TM, w), lambda i: (jnp.maximum(i - 1, 0), 0))
    in_specs = [cur(d), cur(Z_W), cur(RET_V_W),
                _const_spec(dintra.shape), _const_spec(cdec.shape), _const_spec(swa_mask.shape),
                pl.BlockSpec(memory_space=pltpu.SMEM),
                _const_spec((d, d)), _const_spec((1, d)),
                _const_spec((1, d)), _const_spec((1, d)),
                _const_spec((d, D_FF)), _const_spec((d, D_FF)), _const_spec((D_FF, d)),
                prev(PLE_DIM), _const_spec((d, d)), _const_spec((PLE_DIM, d)), _const_spec((1, d))]
    return pl.pallas_call(
        functools.partial(_mix_ffn2_kernel, tiles_per_seq=seq // TM),
        grid=(n_tiles + 1,),
        in_specs=in_specs,
        out_specs=prev(d),
        out_shape=jax.ShapeDtypeStruct((t, d), F32),
        scratch_shapes=[
            pltpu.VMEM((TM, d), F32),
            pltpu.VMEM((TM, d), BF16),
            pltpu.VMEM((TM, d), F32),
            pltpu.VMEM((TM, d), BF16),
            pltpu.VMEM((RET_HEADS, CHUNK, RET_V_DIM), F32),
            pltpu.VMEM((CHUNK, 2 * SWA_KV_W), BF16),
            pltpu.VMEM((CHUNK, 2 * SWA_KV_W), BF16),
        ],
        compiler_params=pltpu.CompilerParams(
            dimension_semantics=("arbitrary",), vmem_limit_bytes=VMEM_LIMIT),
        name="mix_ffn2",
    )(h1, z, rg, dintra, cdec, swa_mask, sinks, w_out, mix_g, pre_g, post_g, wg, wu, wd, p, wpg, wpp, ple_g)


def _tables():
    f32 = np.float32
    half = RET_QK_DIM // 2
    inv_freq = f32(ROPE_BASE) ** (-np.arange(half, dtype=f32) / f32(half))
    invf = np.tile(inv_freq, LANES // half)[None, :]
    sgn = np.tile(np.concatenate([-np.ones(half, f32), np.ones(half, f32)]), LANES // RET_QK_DIM)[None, :]
    log_gamma = np.log(f32(1.0) - f32(2.0) ** (f32(-5.0) - np.arange(RET_HEADS, dtype=f32)))
    idx = np.arange(CHUNK, dtype=f32)
    diff = idx[:, None] - idx[None, :]
    causal = diff >= 0
    dintra = np.where(causal[None], np.exp(log_gamma[:, None, None] * np.where(causal, diff, f32(0.0))[None]),
                      f32(0.0)).astype(f32)
    q_decay = np.exp(log_gamma[:, None] * (idx[None, :] + f32(1.0)))
    k_decay = np.exp(log_gamma[:, None] * (f32(CHUNK - 1.0) - idx[None, :]))
    chunk_decay = np.exp(log_gamma * f32(CHUNK))

    def per_token(dec):
        x = np.repeat(np.transpose(dec)[:, :, None], RET_QK_DIM, axis=2).reshape(CHUNK, RET_QK_W)
        return np.tile(x, (TM // CHUNK, 1)).astype(f32)

    cdec = np.broadcast_to(chunk_decay[:, None, None], (RET_HEADS, CHUNK, RET_V_DIM)).astype(f32)
    qi = np.arange(2 * CHUNK)[:, None] % CHUNK
    kj = np.arange(2 * CHUNK)[None, :]
    band = (kj > qi) & (kj <= qi + CHUNK)
    swa_mask = np.stack([band, band & (kj >= CHUNK)]).astype(f32)
    return invf.astype(f32), sgn, dintra, per_token(q_decay), per_token(k_decay), cdec, swa_mask


def kernel(x, p, positions, ffn1_pre_g, ffn1_post_g, ffn1_w_gate, ffn1_w_up, ffn1_w_down, mix_pre_g, mix_post_g, w_in, b_in, swa_sinks, w_out, ffn2_pre_g, ffn2_post_g, ffn2_w_gate, ffn2_w_up, ffn2_w_down, ple_w_proj, ple_w_gate, ple_norm_g):
    batch, seq, d = x.shape
    depth = p.shape[0]
    assert d == D_MODEL and seq % TM == 0
    t = batch * seq
    h = x.reshape(t, d)
    pos = positions.reshape(t // CHUNK, CHUNK)
    invf, sgn, dintra, qdec, kdec, cdec, swa_mask = _tables()
    bf = lambda w: w.astype(BF16)
    for i in range(depth):
        h1, z, rg, (wg2, wu2, wd2, wo, wpg) = _ffn1_proj_call(
            h, pos, ffn1_pre_g[i][None], ffn1_post_g[i][None],
            bf(ffn1_w_gate[i]), bf(ffn1_w_up[i]), bf(ffn1_w_down[i]),
            mix_pre_g[i][None], bf(w_in[i]), b_in[i][None], invf, sgn, qdec, kdec,
            to_cast=(ffn2_w_gate[i], ffn2_w_up[i], ffn2_w_down[i], w_out[i], ple_w_gate[i]))
        h = _mix_ffn2_call(
            h1, z, rg, dintra, cdec, swa_mask, swa_sinks[i], wo, mix_post_g[i][None],
            ffn2_pre_g[i][None], ffn2_post_g[i][None], wg2, wu2, wd2,
            p[i].reshape(t, PLE_DIM), wpg, bf(ple_w_proj[i]), ple_norm_g[i][None], seq=seq)
    return h.reshape(batch, seq, d)
```
